```python
import math
import jax, jax.numpy as jnp
from jax import lax
import numpy as np

D_MODEL = 1024
BATCH = 8
SEQ = 4096
DEPTH = 2

A_WIDTH = D_MODEL // 2
B_WIDTH = D_MODEL // 2
A_CONV = 3
B_CONV = 31
EVEN_IN = 3 * A_WIDTH + 2 * B_WIDTH

C_HEAD_DIM = 64
C_WIDTH = D_MODEL // 2
C_HEADS = C_WIDTH // (2 * C_HEAD_DIM)
C_QK = C_HEADS * 2 * C_HEAD_DIM
Q_BLOCK = 128
D_WIDTH = D_MODEL // 2
CHUNK = 128
D_GROUP_DIM = 128
D_GROUPS = D_WIDTH // D_GROUP_DIM
ODD_IN = 3 * C_QK + 2 * D_WIDTH

D_FF = -(-8 * D_MODEL // (3 * 256)) * 256

N_EVEN = (DEPTH + 1) // 2
N_ODD = DEPTH // 2
DEEPNORM_ALPHA = (2 * DEPTH) ** 0.25
DEEPNORM_BETA = (8 * DEPTH) ** -0.25
LN_EPS = 1e-5

kernel_name = "hybrid_shortconv_conformer_diffattn_gmlp"


def layer_norm(x, g, b):
    xf = x.astype(jnp.float32)
    mu = jnp.mean(xf, axis=-1, keepdims=True)
    xc = xf - mu
    var = jnp.mean(xc * xc, axis=-1, keepdims=True)
    y = xc * lax.rsqrt(var + LN_EPS) * g.astype(jnp.float32) + b.astype(jnp.float32)
    return y.astype(x.dtype)


def rms_norm(x, g):
    xf = x.astype(jnp.float32)
    y = xf * lax.rsqrt(jnp.mean(xf * xf, axis=-1, keepdims=True) + LN_EPS) * g.astype(jnp.float32)
    return y.astype(x.dtype)


def causal_depthwise_conv(x, w):
    k_width, ch = w.shape
    return lax.conv_general_dilated(
        x, w[:, None, :].astype(x.dtype), window_strides=(1,),
        padding=[(k_width - 1, 0)], dimension_numbers=("NWC", "WIO", "NWC"),
        feature_group_count=ch)


def shortconv_conformer_mixer(x, w_in, conv_a_w, conv_b_w, conv_b_bias, conv_ln_g, conv_ln_b, w_out):
    proj = x @ w_in.astype(x.dtype)
    gate_b, gate_c, h, glu_a, glu_g = jnp.split(
        proj, [A_WIDTH, 2 * A_WIDTH, 3 * A_WIDTH, 3 * A_WIDTH + B_WIDTH], axis=-1)
    y_a = gate_b * causal_depthwise_conv(gate_c * h, conv_a_w)
    z = glu_a * jax.nn.sigmoid(glu_g)
    z = causal_depthwise_conv(z, conv_b_w) + conv_b_bias.astype(x.dtype)
    y_b = jax.nn.silu(layer_norm(z, conv_ln_g, conv_ln_b))
    return jnp.concatenate([y_a, y_b], axis=-1) @ w_out.astype(x.dtype)


def diffattn_gmlp_mixer(x, w_in, lambda_q1, lambda_k1, lambda_q2, lambda_k2, subln_g,
                        gmlp_ln_g, gmlp_ln_b, spatial_w, spatial_b, w_out, lambda_init):
    bsz, seq, _ = x.shape
    proj = x @ w_in.astype(x.dtype)
    q, k, v, uv = jnp.split(proj, [C_QK, 2 * C_QK, 3 * C_QK], axis=-1)
    q = q.reshape(bsz, seq, C_HEADS, 2, C_HEAD_DIM)
    k = k.reshape(bsz, seq, C_HEADS, 2, C_HEAD_DIM)
    v = v.reshape(bsz, seq, C_HEADS, 2 * C_HEAD_DIM)

    lam = (jnp.exp(jnp.sum(lambda_q1.astype(jnp.float32) * lambda_k1.astype(jnp.float32)))
           - jnp.exp(jnp.sum(lambda_q2.astype(jnp.float32) * lambda_k2.astype(jnp.float32)))
           + lambda_init)
    scale = C_HEAD_DIM ** -0.5
    n_blocks = seq // Q_BLOCK
    q_blocks = q.reshape(bsz, n_blocks, Q_BLOCK, C_HEADS, 2, C_HEAD_DIM).transpose(1, 0, 2, 3, 4, 5)
    key_pos = jnp.arange(seq)

    def attend(args):
        q_blk, blk = args
        s = jnp.einsum("bqhcd,bkhcd->bhcqk", q_blk, k).astype(jnp.float32) * scale
        q_pos = blk * Q_BLOCK + jnp.arange(Q_BLOCK)
        mask = key_pos[None, :] <= q_pos[:, None]
        s = jnp.where(mask, s, -jnp.inf)
        p = jax.nn.softmax(s, axis=-1)
        a = p[:, :, 0] - lam * p[:, :, 1]
        return jnp.einsum("bhqk,bkhe->bqhe", a.astype(v.dtype), v)

    o = lax.map(attend, (q_blocks, jnp.arange(n_blocks)))
    o = o.transpose(1, 0, 2, 3, 4).reshape(bsz, seq, C_HEADS, 2 * C_HEAD_DIM)
    o = rms_norm(o, subln_g) * (1.0 - lambda_init)
    y_c = o.reshape(bsz, seq, C_WIDTH)

    z = jax.nn.gelu(uv)
    u, vg = jnp.split(z, 2, axis=-1)
    vg = vg.reshape(bsz, seq // CHUNK, CHUNK, D_GROUPS, D_GROUP_DIM)
    vg = layer_norm(vg, gmlp_ln_g, gmlp_ln_b)
    tri = jnp.tril(jnp.ones((CHUNK, CHUNK), dtype=bool))
    w_causal = jnp.where(tri[None], spatial_w, 0.0).astype(x.dtype)
    sp = jnp.einsum("gts,bcsgd->bctgd", w_causal, vg) + spatial_b.T.astype(x.dtype)[:, :, None]
    y_d = u * sp.reshape(bsz, seq, D_WIDTH)

    return jnp.concatenate([y_c, y_d], axis=-1) @ w_out.astype(x.dtype)


def swiglu(x, w_gate, w_up, w_down):
    return (jax.nn.silu(x @ w_gate.astype(x.dtype)) * (x @ w_up.astype(x.dtype))) @ w_down.astype(x.dtype)


def setup_inputs(seed: int = 0) -> dict:
    key = jax.random.key(seed)
    ks = jax.random.split(key, 26)
    f32 = jnp.float32
    nrm = lambda k, shape, s: jax.random.normal(k, shape, f32) * s
    mix_w = 2 * D_MODEL // 2
    return {
        "x": nrm(ks[0], (BATCH, SEQ, D_MODEL), 1.0),
        "even_w_in": nrm(ks[1], (N_EVEN, D_MODEL, EVEN_IN), D_MODEL ** -0.5),
        "even_conv_a_w": nrm(ks[2], (N_EVEN, A_CONV, A_WIDTH), A_CONV ** -0.5),
        "even_conv_b_w": nrm(ks[3], (N_EVEN, B_CONV, B_WIDTH), B_CONV ** -0.5),
        "even_conv_b_bias": nrm(ks[4], (N_EVEN, B_WIDTH), 0.01),
        "even_conv_ln_g": 1.0 + nrm(ks[5], (N_EVEN, B_WIDTH), 0.05),
        "even_conv_ln_b": nrm(ks[6], (N_EVEN, B_WIDTH), 0.01),
        "even_w_out": nrm(ks[7], (N_EVEN, mix_w, D_MODEL), mix_w ** -0.5 * DEEPNORM_BETA),
        "odd_w_in": nrm(ks[8], (N_ODD, D_MODEL, ODD_IN), D_MODEL ** -0.5),
        "odd_lambda_q1": nrm(ks[9], (N_ODD, C_HEAD_DIM), 0.1),
        "odd_lambda_k1": nrm(ks[10], (N_ODD, C_HEAD_DIM), 0.1),
        "odd_lambda_q2": nrm(ks[11], (N_ODD, C_HEAD_DIM), 0.1),
        "odd_lambda_k2": nrm(ks[12], (N_ODD, C_HEAD_DIM), 0.1),
        "odd_subln_g": 1.0 + nrm(ks[13], (N_ODD, 2 * C_HEAD_DIM), 0.05),
        "odd_gmlp_ln_g": 1.0 + nrm(ks[14], (N_ODD, D_GROUPS, D_GROUP_DIM), 0.05),
        "odd_gmlp_ln_b": nrm(ks[15], (N_ODD, D_GROUPS, D_GROUP_DIM), 0.01),
        "odd_spatial_w": nrm(ks[16], (N_ODD, D_GROUPS, CHUNK, CHUNK), CHUNK ** -0.5),
        "odd_spatial_b": 1.0 + nrm(ks[17], (N_ODD, D_GROUPS, CHUNK), 0.05),
        "odd_w_out": nrm(ks[18], (N_ODD, mix_w, D_MODEL), mix_w ** -0.5 * DEEPNORM_BETA),
        "mix_ln_g": 1.0 + nrm(ks[19], (DEPTH, D_MODEL), 0.05),
        "mix_ln_b": nrm(ks[20], (DEPTH, D_MODEL), 0.01),
        "ffn_w_gate": nrm(ks[21], (DEPTH, D_MODEL, D_FF), D_MODEL ** -0.5),
        "ffn_w_up": nrm(ks[22], (DEPTH, D_MODEL, D_FF), D_MODEL ** -0.5),
        "ffn_w_down": nrm(ks[23], (DEPTH, D_FF, D_MODEL), D_FF ** -0.5 * DEEPNORM_BETA),
        "ffn_ln_g": 1.0 + nrm(ks[24], (DEPTH, D_MODEL), 0.05),
        "ffn_ln_b": nrm(ks[25], (DEPTH, D_MODEL), 0.01),
    }


def reference(x, even_w_in, even_conv_a_w, even_conv_b_w, even_conv_b_bias, even_conv_ln_g,
              even_conv_ln_b, even_w_out, odd_w_in, odd_lambda_q1, odd_lambda_k1, odd_lambda_q2,
              odd_lambda_k2, odd_subln_g, odd_gmlp_ln_g, odd_gmlp_ln_b, odd_spatial_w, odd_spatial_b,
              odd_w_out, mix_ln_g, mix_ln_b, ffn_w_gate, ffn_w_up, ffn_w_down, ffn_ln_g, ffn_ln_b):
    for layer in range(DEPTH):
        j = layer // 2
        if layer % 2 == 0:
            m = shortconv_conformer_mixer(x, even_w_in[j], even_conv_a_w[j], even_conv_b_w[j],
                                          even_conv_b_bias[j], even_conv_ln_g[j], even_conv_ln_b[j],
                                          even_w_out[j])
        else:
            lambda_init = 0.8 - 0.6 * math.exp(-0.3 * layer)
            m = diffattn_gmlp_mixer(x, odd_w_in[j], odd_lambda_q1[j], odd_lambda_k1[j],
                                    odd_lambda_q2[j], odd_lambda_k2[j], odd_subln_g[j],
                                    odd_gmlp_ln_g[j], odd_gmlp_ln_b[j], odd_spatial_w[j],
                                    odd_spatial_b[j], odd_w_out[j], lambda_init)
        x = layer_norm(DEEPNORM_ALPHA * x + m, mix_ln_g[layer], mix_ln_b[layer])
        f = swiglu(x, ffn_w_gate[layer], ffn_w_up[layer], ffn_w_down[layer])
        x = layer_norm(DEEPNORM_ALPHA * x + f, ffn_ln_g[layer], ffn_ln_b[layer])
    return x
```

```python
import functools
import math

import jax
import jax.numpy as jnp
from jax import lax
from jax.experimental import pallas as pl
from jax.experimental.pallas import tpu as pltpu

F32 = jnp.float32
BF16 = jnp.bfloat16

D_MODEL = 1024
HALF = D_MODEL // 2
A_CONV = 3
B_CONV = 31
C_HEAD_DIM = 64
C_HEADS = 4
HEAD_W = 2 * C_HEAD_DIM
CHUNK = 128
D_GROUPS = 4
D_FF = 2816
DEPTH = 2
ALPHA = (2 * DEPTH) ** 0.25
LN_EPS = 1e-5
LOG2E = 1.4426950408889634

HALO_A = 8
HALO_B = 32
CONV_ROWS = 32

TILE = 512
FF_CHUNK = 256
TQ = 256
VMEM_LIMIT = 56 * 1024 * 1024


def _layer_norm(x, g, b):
    mu = jnp.mean(x, axis=-1, keepdims=True)
    xc = x - mu
    var = jnp.mean(xc * xc, axis=-1, keepdims=True)
    return xc * lax.rsqrt(var + LN_EPS) * g + b


def _sigmoid(x):
    return 1.0 / (1.0 + jnp.exp(-x))


def _dot(a, b):
    return jnp.dot(a, b, preferred_element_type=F32)


def _const_spec(shape):
    zeros = (0,) * len(shape)
    return pl.BlockSpec(shape, lambda *_: zeros, pipeline_mode=pl.Buffered(1))


def _causal_conv(ext_ref, w_ref, taps, first_row, rows, out_fn):
    for r0 in range(0, rows, CONV_ROWS):
        acc = w_ref[0:1, :] * ext_ref[pl.ds(first_row + r0, CONV_ROWS), :]
        for k in range(1, taps):
            acc = acc + w_ref[k:k + 1, :] * ext_ref[pl.ds(first_row + r0 + k, CONV_ROWS), :]
        out_fn(r0, acc)


def _even_mixer_kernel(x_ref, w_in_ref, wa_ref, wb_ref, bias_ref, g_ref, b_ref,
                       ya_ref, yb_ref, aext, zext, gate_scr, conv_scr):
    rows = x_ref.shape[0]

    @pl.when(pl.program_id(1) == 0)
    def _():
        aext[0:HALO_A, :] = jnp.zeros((HALO_A, HALF), F32)
        zext[0:HALO_B, :] = jnp.zeros((HALO_B, HALF), F32)

    xb = x_ref[...].astype(BF16)

    def proj(c):
        return _dot(xb, w_in_ref[:, c * HALF:(c + 1) * HALF])

    gate_scr[...] = proj(0)
    aext[HALO_A:HALO_A + rows, :] = proj(1) * proj(2)

    def store_a(r0, acc):
        ya_ref[pl.ds(r0, CONV_ROWS), :] = (gate_scr[pl.ds(r0, CONV_ROWS), :] * acc).astype(BF16)

    _causal_conv(aext, wa_ref, A_CONV, HALO_A - (A_CONV - 1), rows, store_a)
    aext[0:HALO_A, :] = aext[rows:rows + HALO_A, :]

    zext[HALO_B:HALO_B + rows, :] = proj(3) * _sigmoid(proj(4))

    def store_b(r0, acc):
        conv_scr[pl.ds(r0, CONV_ROWS), :] = acc

    _causal_conv(zext, wb_ref, B_CONV, HALO_B - (B_CONV - 1), rows, store_b)
    zext[0:HALO_B, :] = zext[rows:rows + HALO_B, :]

    zn = _layer_norm(conv_scr[...] + bias_ref[...], g_ref[...], b_ref[...])
    yb_ref[...] = (zn * _sigmoid(zn)).astype(BF16)


def _even_mixer(x, w_in, wa, wb, bias, g, b, batch, seq):
    n = x.shape[0]
    steps = seq // TILE
    row_spec = lambda w: pl.BlockSpec((TILE, w), lambda bi, si: (bi * steps + si, 0))
    return pl.pallas_call(
        _even_mixer_kernel,
        grid=(batch, steps),
        in_specs=[row_spec(D_MODEL), _const_spec(w_in.shape), _const_spec(wa.shape),
                  _const_spec(wb.shape), _const_spec(bias.shape), _const_spec(g.shape),
                  _const_spec(b.shape)],
        out_specs=[row_spec(HALF), row_spec(HALF)],
        out_shape=[jax.ShapeDtypeStruct((n, HALF), BF16)] * 2,
        scratch_shapes=[pltpu.VMEM((TILE + HALO_A, HALF), F32),
                        pltpu.VMEM((TILE + HALO_B, HALF), F32),
                        pltpu.VMEM((TILE, HALF), F32),
                        pltpu.VMEM((TILE, HALF), F32)],
        compiler_params=pltpu.CompilerParams(
            dimension_semantics=("arbitrary", "arbitrary"), vmem_limit_bytes=VMEM_LIMIT),
        name="even_mixer",
    )(x, w_in, wa, wb, bias, g, b)


def _post_kernel(ya_ref, yb_ref, x_ref, wout_ref, mg_ref, mb_ref, wg_ref, wu_ref, wd_ref,
                 fg_ref, fb_ref, o_ref, h_scr):
    m = _dot(ya_ref[...], wout_ref[0:HALF, :]) + _dot(yb_ref[...], wout_ref[HALF:D_MODEL, :])
    x1 = _layer_norm(ALPHA * x_ref[...] + m, mg_ref[...], mb_ref[...])
    xb = x1.astype(BF16)
    for c in range(0, D_FF, FF_CHUNK):
        gate = _dot(xb, wg_ref[:, c:c + FF_CHUNK])
        up = _dot(xb, wu_ref[:, c:c + FF_CHUNK])
        h_scr[:, c:c + FF_CHUNK] = (gate * _sigmoid(gate) * up).astype(BF16)
    f = _dot(h_scr[...], wd_ref[...])
    o_ref[...] = _layer_norm(ALPHA * x1 + f, fg_ref[...], fb_ref[...])


def _post_block(ya, yb, x, wout, mg, mb, wg, wu, wd, fg, fb):
    n = x.shape[0]
    row_spec = lambda w: pl.BlockSpec((TILE, w), lambda i: (i, 0))
    return pl.pallas_call(
        _post_kernel,
        grid=(n // TILE,),
        in_specs=[row_spec(HALF), row_spec(HALF), row_spec(D_MODEL), _const_spec(wout.shape),
                  _const_spec(mg.shape), _const_spec(mb.shape), _const_spec(wg.shape),
                  _const_spec(wu.shape), _const_spec(wd.shape), _const_spec(fg.shape),
                  _const_spec(fb.shape)],
        out_specs=row_spec(D_MODEL),
        out_shape=jax.ShapeDtypeStruct((n, D_MODEL), F32),
        scratch_shapes=[pltpu.VMEM((TILE, D_FF), BF16)],
        compiler_params=pltpu.CompilerParams(
            dimension_semantics=("arbitrary",), vmem_limit_bytes=VMEM_LIMIT),
        name="post_block",
    )(ya, yb, x, wout, mg, mb, wg, wu, wd, fg, fb)


def _gelu_tanh(x):
    return 0.5 * x * (1.0 + jnp.tanh(math.sqrt(2.0 / math.pi) * (x + 0.044715 * (x * x * x))))


def _odd_proj_kernel(x_ref, w_in_ref, lg_ref, lb_ref, sw_ref, sb_ref,
                     q_ref, k_ref, v_ref, yd_ref):
    rows = x_ref.shape[0]
    xb = x_ref[...].astype(BF16)

    def proj(c):
        return _dot(xb, w_in_ref[:, c * HALF:(c + 1) * HALF])

    q_ref[...] = (proj(0) * (C_HEAD_DIM ** -0.5 * LOG2E)).astype(BF16)
    k_ref[...] = proj(1).astype(BF16)
    v_ref[...] = proj(2).astype(BF16)

    u = _gelu_tanh(proj(3))
    vg = _gelu_tanh(proj(4))
    row = lax.broadcasted_iota(jnp.int32, (CHUNK, CHUNK), 0)
    col = lax.broadcasted_iota(jnp.int32, (CHUNK, CHUNK), 1)
    for g in range(D_GROUPS):
        lanes = slice(g * CHUNK, (g + 1) * CHUNK)
        vn = _layer_norm(vg[:, lanes], lg_ref[:, lanes], lb_ref[:, lanes]).astype(BF16)
        w_causal = jnp.where(col <= row, sw_ref[g], 0.0).astype(BF16)
        for r0 in range(0, rows, CHUNK):
            sp = _dot(w_causal, vn[r0:r0 + CHUNK, :]) + sb_ref[:, lanes]
            yd_ref[r0:r0 + CHUNK, lanes] = (u[r0:r0 + CHUNK, lanes] * sp).astype(BF16)


def _odd_proj(x, w_in, lg, lb, sw, sb):
    n = x.shape[0]
    row_spec = lambda w: pl.BlockSpec((TILE, w), lambda i: (i, 0))
    return pl.pallas_call(
        _odd_proj_kernel,
        grid=(n // TILE,),
        in_specs=[row_spec(D_MODEL), _const_spec(w_in.shape), _const_spec(lg.shape),
                  _const_spec(lb.shape), _const_spec(sw.shape), _const_spec(sb.shape)],
        out_specs=[row_spec(HALF)] * 4,
        out_shape=[jax.ShapeDtypeStruct((n, HALF), BF16)] * 4,
        compiler_params=pltpu.CompilerParams(
            dimension_semantics=("arbitrary",), vmem_limit_bytes=VMEM_LIMIT),
        name="odd_proj",
    )(x, w_in, lg, lb, sw, sb)


def _attn_kernel(q_ref, k_ref, v_ref, lq1_ref, lk1_ref, lq2_ref, lk2_ref, sg_ref, o_ref,
                 *, lambda_init):
    qi = pl.program_id(2)
    q = q_ref[...]
    lane = lax.broadcasted_iota(jnp.int32, q.shape, 1)
    zero = jnp.zeros_like(q)
    qs = jnp.concatenate([jnp.where(lane < C_HEAD_DIM, q, zero),
                          jnp.where(lane >= C_HEAD_DIM, q, zero)], axis=0)

    def step(j, carry, masked):
        m, l, acc = carry
        start = pl.multiple_of(j * TQ, TQ)
        kj = k_ref[pl.ds(start, TQ), :]
        vj = v_ref[pl.ds(start, TQ), :]
        s = lax.dot_general(qs, kj, (((1,), (1,)), ((), ())), preferred_element_type=F32)
        if masked:
            r = lax.broadcasted_iota(jnp.int32, (TQ, TQ), 0)
            c = lax.broadcasted_iota(jnp.int32, (TQ, TQ), 1)
            keep = jnp.concatenate([c <= r, c <= r], axis=0)
            s = jnp.where(keep, s, -jnp.inf)
        m_new = jnp.maximum(m, jnp.max(s, axis=-1, keepdims=True))
        p = jnp.exp2(s - m_new)
        a = jnp.exp2(m - m_new)
        l = a * l + jnp.sum(p, axis=-1, keepdims=True)
        acc = a * acc + _dot(p.astype(BF16), vj)
        return m_new, l, acc

    init = (jnp.full((2 * TQ, 1), -jnp.inf, F32), jnp.zeros((2 * TQ, 1), F32),
            jnp.zeros((2 * TQ, HEAD_W), F32))
    carry = lax.fori_loop(0, qi, lambda j, c: step(j, c, False), init)
    _, l, acc = step(qi, carry, True)

    o = acc / l
    lam = (jnp.exp(jnp.sum(lq1_ref[...] * lk1_ref[...], axis=-1, keepdims=True))
           - jnp.exp(jnp.sum(lq2_ref[...] * lk2_ref[...], axis=-1, keepdims=True))
           + lambda_init)
    o = o[0:TQ, :] - lam * o[TQ:2 * TQ, :]
    o = o * lax.rsqrt(jnp.mean(o * o, axis=-1, keepdims=True) + LN_EPS) * sg_ref[...]
    o_ref[...] = (o * (1.0 - lambda_init)).astype(BF16)


def _diff_attention(q, k, v, lq1, lk1, lq2, lk2, sg, batch, seq, lambda_init):
    n = q.shape[0]
    steps = seq // TQ
    q_spec = pl.BlockSpec((TQ, HEAD_W), lambda b, h, i: (b * steps + i, h))
    kv_spec = pl.BlockSpec((seq, HEAD_W), lambda b, h, i: (b, h))
    return pl.pallas_call(
        functools.partial(_attn_kernel, lambda_init=lambda_init),
        grid=(batch, C_HEADS, steps),
        in_specs=[q_spec, kv_spec, kv_spec, _const_spec(lq1.shape), _const_spec(lk1.shape),
                  _const_spec(lq2.shape), _const_spec(lk2.shape), _const_spec(sg.shape)],
        out_specs=q_spec,
        out_shape=jax.ShapeDtypeStruct((n, HALF), BF16),
        compiler_params=pltpu.CompilerParams(
            dimension_semantics=("arbitrary", "arbitrary", "arbitrary"),
            vmem_limit_bytes=VMEM_LIMIT),
        name="diff_attention",
    )(q, k, v, lq1, lk1, lq2, lk2, sg)


def kernel(x, even_w_in, even_conv_a_w, even_conv_b_w, even_conv_b_bias, even_conv_ln_g, even_conv_ln_b, even_w_out, odd_w_in, odd_lambda_q1, odd_lambda_k1, odd_lambda_q2, odd_lambda_k2, odd_subln_g, odd_gmlp_ln_g, odd_gmlp_ln_b, odd_spatial_w, odd_spatial_b, odd_w_out, mix_ln_g, mix_ln_b, ffn_w_gate, ffn_w_up, ffn_w_down, ffn_ln_g, ffn_ln_b):
    batch, seq, d = x.shape
    assert d == D_MODEL and seq % TILE == 0 and seq % TQ == 0
    xf = x.reshape(batch * seq, d)
    row = lambda a: a.reshape(1, -1)
    bf = lambda a: a.astype(BF16)

    def post(layer, ya, yb, xres, w_out):
        return _post_block(ya, yb, xres, bf(w_out), row(mix_ln_g[layer]), row(mix_ln_b[layer]),
                           bf(ffn_w_gate[layer]), bf(ffn_w_up[layer]), bf(ffn_w_down[layer]),
                           row(ffn_ln_g[layer]), row(ffn_ln_b[layer]))

    ya, yb = _even_mixer(xf, bf(even_w_in[0]), even_conv_a_w[0], even_conv_b_w[0],
                         row(even_conv_b_bias[0]), row(even_conv_ln_g[0]), row(even_conv_ln_b[0]),
                         batch, seq)
    xf = post(0, ya, yb, xf, even_w_out[0])

    lambda_init = 0.8 - 0.6 * math.exp(-0.3 * 1)
    sb = jnp.broadcast_to(odd_spatial_b[0].T[:, :, None], (CHUNK, D_GROUPS, CHUNK)).reshape(CHUNK, HALF)
    q, k, v, yd = _odd_proj(xf, bf(odd_w_in[0]), row(odd_gmlp_ln_g[0]), row(odd_gmlp_ln_b[0]),
                            odd_spatial_w[0], sb)
    yc = _diff_attention(q, k, v, row(odd_lambda_q1[0]), row(odd_lambda_k1[0]),
                         row(odd_lambda_q2[0]), row(odd_lambda_k2[0]), row(odd_subln_g[0]),
                         batch, seq, lambda_init)
    xf = post(1, yc, yd, xf, odd_w_out[0])
    return xf.reshape(batch, seq, d)
```

```python
import functools
import math

import jax
import jax.numpy as jnp
from jax import lax
from jax.experimental import pallas as pl
from jax.experimental.pallas import tpu as pltpu

F32 = jnp.float32
BF16 = jnp.bfloat16

D_MODEL = 1024
HALF = D_MODEL // 2
A_CONV = 3
B_CONV = 31
C_HEAD_DIM = 64
C_HEADS = 4
HEAD_W = 2 * C_HEAD_DIM
CHUNK = 128
D_GROUPS = 4
D_FF = 2816
DEPTH = 2
ALPHA = (2 * DEPTH) ** 0.25
LN_EPS = 1e-5
LOG2E = 1.4426950408889634

HALO_A = 8
HALO_B = 32
CONV_ROWS = 32

TILE = 512
FF_CHUNK = 256
TQ = 256
VMEM_LIMIT = 56 * 1024 * 1024


def _layer_norm(x, g, b):
    mu = jnp.mean(x, axis=-1, keepdims=True)
    xc = x - mu
    var = jnp.mean(xc * xc, axis=-1, keepdims=True)
    return xc * lax.rsqrt(var + LN_EPS) * g + b


def _sigmoid(x):
    return 1.0 / (1.0 + jnp.exp(-x))


def _dot(a, b):
    return jnp.dot(a, b, preferred_element_type=F32)


def _const_spec(shape):
    zeros = (0,) * len(shape)
    return pl.BlockSpec(shape, lambda *_: zeros, pipeline_mode=pl.Buffered(1))


def _causal_conv(ext_ref, w_ref, taps, first_row, rows, out_fn):
    for r0 in range(0, rows, CONV_ROWS):
        acc = w_ref[0:1, :] * ext_ref[pl.ds(first_row + r0, CONV_ROWS), :]
        for k in range(1, taps):
            acc = acc + w_ref[k:k + 1, :] * ext_ref[pl.ds(first_row + r0 + k, CONV_ROWS), :]
        out_fn(r0, acc)


def _even_mixer_kernel(x_ref, w_in_ref, wa_ref, wb_ref, bias_ref, g_ref, b_ref,
                       ya_ref, yb_ref, aext, zext, gate_scr, conv_scr):
    rows = x_ref.shape[0]

    @pl.when(pl.program_id(1) == 0)
    def _():
        aext[0:HALO_A, :] = jnp.zeros((HALO_A, HALF), F32)
        zext[0:HALO_B, :] = jnp.zeros((HALO_B, HALF), F32)

    xb = x_ref[...].astype(BF16)

    def proj(c):
        return _dot(xb, w_in_ref[:, c * HALF:(c + 1) * HALF])

    gate_scr[...] = proj(0)
    aext[HALO_A:HALO_A + rows, :] = proj(1) * proj(2)

    def store_a(r0, acc):
        ya_ref[pl.ds(r0, CONV_ROWS), :] = (gate_scr[pl.ds(r0, CONV_ROWS), :] * acc).astype(BF16)

    _causal_conv(aext, wa_ref, A_CONV, HALO_A - (A_CONV - 1), rows, store_a)
    aext[0:HALO_A, :] = aext[rows:rows + HALO_A, :]

    zext[HALO_B:HALO_B + rows, :] = proj(3) * _sigmoid(proj(4))

    def store_b(r0, acc):
        conv_scr[pl.ds(r0, CONV_ROWS), :] = acc

    _causal_conv(zext, wb_ref, B_CONV, HALO_B - (B_CONV - 1), rows, store_b)
    zext[0:HALO_B, :] = zext[rows:rows + HALO_B, :]

    zn = _layer_norm(conv_scr[...] + bias_ref[...], g_ref[...], b_ref[...])
    yb_ref[...] = (zn * _sigmoid(zn)).astype(BF16)


def _even_mixer(x, w_in, wa, wb, bias, g, b, batch, seq):
    n = x.shape[0]
    steps = seq // TILE
    row_spec = lambda w: pl.BlockSpec((TILE, w), lambda bi, si: (bi * steps + si, 0))
    return pl.pallas_call(
        _even_mixer_kernel,
        grid=(batch, steps),
        in_specs=[row_spec(D_MODEL), _const_spec(w_in.shape), _const_spec(wa.shape),
                  _const_spec(wb.shape), _const_spec(bias.shape), _const_spec(g.shape),
                  _const_spec(b.shape)],
        out_specs=[row_spec(HALF), row_spec(HALF)],
        out_shape=[jax.ShapeDtypeStruct((n, HALF), BF16)] * 2,
        scratch_shapes=[pltpu.VMEM((TILE + HALO_A, HALF), F32),
                        pltpu.VMEM((TILE + HALO_B, HALF), F32),
                        pltpu.VMEM((TILE, HALF), F32),
                        pltpu.VMEM((TILE, HALF), F32)],
        compiler_params=pltpu.CompilerParams(
            dimension_semantics=("arbitrary", "arbitrary"), vmem_limit_bytes=VMEM_LIMIT),
        name="even_mixer",
    )(x, w_in, wa, wb, bias, g, b)


def _post_kernel(ya_ref, yb_ref, x_ref, wout_ref, mg_ref, mb_ref, wg_ref, wu_ref, wd_ref,
                 fg_ref, fb_ref, o_ref, h_scr):
    m = _dot(ya_ref[...], wout_ref[0:HALF, :]) + _dot(yb_ref[...], wout_ref[HALF:D_MODEL, :])
    x1 = _layer_norm(ALPHA * x_ref[...] + m, mg_ref[...], mb_ref[...])
    xb = x1.astype(BF16)
    for c in range(0, D_FF, FF_CHUNK):
        gate = _dot(xb, wg_ref[:, c:c + FF_CHUNK])
        up = _dot(xb, wu_ref[:, c:c + FF_CHUNK])
        h_scr[:, c:c + FF_CHUNK] = (gate * _sigmoid(gate) * up).astype(BF16)
    f = _dot(h_scr[...], wd_ref[...])
    o_ref[...] = _layer_norm(ALPHA * x1 + f, fg_ref[...], fb_ref[...])


def _post_block(ya, yb, x, wout, mg, mb, wg, wu, wd, fg, fb):
    n = x.shape[0]
    row_spec = lambda w: pl.BlockSpec((TILE, w), lambda i: (i, 0))
    return pl.pallas_call(
        _post_kernel,
        grid=(n // TILE,),
        in_specs=[row_spec(HALF), row_spec(HALF), row_spec(D_MODEL), _const_spec(wout.shape),
                  _const_spec(mg.shape), _const_spec(mb.shape), _const_spec(wg.shape),
                  _const_spec(wu.shape), _const_spec(wd.shape), _const_spec(fg.shape),
                  _const_spec(fb.shape)],
        out_specs=row_spec(D_MODEL),
        out_shape=jax.ShapeDtypeStruct((n, D_MODEL), F32),
        scratch_shapes=[pltpu.VMEM((TILE, D_FF), BF16)],
        compiler_params=pltpu.CompilerParams(
            dimension_semantics=("arbitrary",), vmem_limit_bytes=VMEM_LIMIT),
        name="post_block",
    )(ya, yb, x, wout, mg, mb, wg, wu, wd, fg, fb)


def _gelu_tanh(x):
    return 0.5 * x * (1.0 + jnp.tanh(math.sqrt(2.0 / math.pi) * (x + 0.044715 * (x * x * x))))


def _dot_nt(a, b):
    return lax.dot_general(a, b, (((1,), (1,)), ((), ())), preferred_element_type=F32)


def _odd_proj_kernel(x_ref, w_in_ref, wqv_t_ref, lg_ref, lb_ref, sw_ref, sb_ref,
                     qt_ref, k_ref, vt_ref, yd_ref):
    rows = x_ref.shape[0]
    xb = x_ref[...].astype(BF16)

    def proj(c):
        return _dot(xb, w_in_ref[:, c * HALF:(c + 1) * HALF])

    q_t = (_dot_nt(wqv_t_ref[0:HALF, :], xb) * (C_HEAD_DIM ** -0.5 * LOG2E)).astype(BF16)
    v_t = _dot_nt(wqv_t_ref[HALF:2 * HALF, :], xb).astype(BF16)
    for t in range(rows // TQ):
        qt_ref[t] = q_t[:, t * TQ:(t + 1) * TQ]
        vt_ref[t] = v_t[:, t * TQ:(t + 1) * TQ]
    k_ref[...] = proj(1).astype(BF16)

    u = _gelu_tanh(proj(3))
    vg = _gelu_tanh(proj(4))
    row = lax.broadcasted_iota(jnp.int32, (CHUNK, CHUNK), 0)
    col = lax.broadcasted_iota(jnp.int32, (CHUNK, CHUNK), 1)
    for g in range(D_GROUPS):
        lanes = slice(g * CHUNK, (g + 1) * CHUNK)
        vn = _layer_norm(vg[:, lanes], lg_ref[:, lanes], lb_ref[:, lanes]).astype(BF16)
        w_causal = jnp.where(col <= row, sw_ref[g], 0.0).astype(BF16)
        for r0 in range(0, rows, CHUNK):
            sp = _dot(w_causal, vn[r0:r0 + CHUNK, :]) + sb_ref[:, lanes]
            yd_ref[r0:r0 + CHUNK, lanes] = (u[r0:r0 + CHUNK, lanes] * sp).astype(BF16)


def _odd_proj(x, w_in, wqv_t, lg, lb, sw, sb):
    n = x.shape[0]
    row_spec = lambda w: pl.BlockSpec((TILE, w), lambda i: (i, 0))
    t_spec = pl.BlockSpec((TILE // TQ, HALF, TQ), lambda i: (i, 0, 0))
    t_shape = jax.ShapeDtypeStruct((n // TQ, HALF, TQ), BF16)
    r_shape = jax.ShapeDtypeStruct((n, HALF), BF16)
    return pl.pallas_call(
        _odd_proj_kernel,
        grid=(n // TILE,),
        in_specs=[row_spec(D_MODEL), _const_spec(w_in.shape), _const_spec(wqv_t.shape),
                  _const_spec(lg.shape), _const_spec(lb.shape), _const_spec(sw.shape),
                  _const_spec(sb.shape)],
        out_specs=[t_spec, row_spec(HALF), t_spec, row_spec(HALF)],
        out_shape=[t_shape, r_shape, t_shape, r_shape],
        compiler_params=pltpu.CompilerParams(
            dimension_semantics=("arbitrary",), vmem_limit_bytes=VMEM_LIMIT),
        name="odd_proj",
    )(x, w_in, wqv_t, lg, lb, sw, sb)


ONES_ROWS = 16


def _attn_kernel(qt_ref, k_ref, vt_ref, lq1_ref, lk1_ref, lq2_ref, lk2_ref, sg_ref, o_ref,
                 qs_scr, m_scr, acc_scr, *, lambda_init):
    qi = pl.program_id(1)
    chan = lax.broadcasted_iota(jnp.int32, (HEAD_W, TQ), 0)
    zero = jnp.zeros((HEAD_W, TQ), BF16)
    for h in range(C_HEADS):
        q_t = qt_ref[h * HEAD_W:(h + 1) * HEAD_W, :]
        qs_scr[h, :, 0:TQ] = jnp.where(chan < C_HEAD_DIM, q_t, zero)
        qs_scr[h, :, TQ:2 * TQ] = jnp.where(chan >= C_HEAD_DIM, q_t, zero)
    m_scr[...] = jnp.full(m_scr.shape, -jnp.inf, F32)
    acc_scr[...] = jnp.zeros(acc_scr.shape, F32)
    ones = jnp.ones((ONES_ROWS, TQ), BF16)

    def step(j, masked):
        rows = pl.ds(pl.multiple_of(j * TQ, TQ), TQ)
        heads = range(C_HEADS)
        s_t = [_dot(k_ref[rows, h * HEAD_W:(h + 1) * HEAD_W], qs_scr[h]) for h in heads]
        if masked:
            key = lax.broadcasted_iota(jnp.int32, (TQ, TQ), 0)
            qry = lax.broadcasted_iota(jnp.int32, (TQ, TQ), 1)
            keep = jnp.concatenate([key <= qry, key <= qry], axis=1)
            s_t = [jnp.where(keep, s, -jnp.inf) for s in s_t]
        m_old = [m_scr[h] for h in heads]
        m_new = [jnp.maximum(m_old[h], jnp.max(s_t[h], axis=0, keepdims=True)) for h in heads]
        p_t = [jnp.exp2(s_t[h] - m_new[h]).astype(BF16) for h in heads]
        pv = [_dot(jnp.concatenate([vt_ref[j, h * HEAD_W:(h + 1) * HEAD_W, :], ones], axis=0),
                   p_t[h]) for h in heads]
        for h in heads:
            acc_scr[h] = jnp.exp2(m_old[h] - m_new[h]) * acc_scr[h] + pv[h]
            m_scr[h] = m_new[h]

    def body(j, carry):
        step(j, False)
        return carry

    lax.fori_loop(0, qi, body, 0)
    step(qi, True)

    lam = (jnp.exp(jnp.sum(lq1_ref[...] * lk1_ref[...], axis=-1, keepdims=True))
           - jnp.exp(jnp.sum(lq2_ref[...] * lk2_ref[...], axis=-1, keepdims=True))
           + lambda_init)
    for h in range(C_HEADS):
        o = acc_scr[h, 0:HEAD_W, :] / acc_scr[h, HEAD_W:HEAD_W + 1, :]
        o = o[:, 0:TQ] - lam * o[:, TQ:2 * TQ]
        o = o * lax.rsqrt(jnp.mean(o * o, axis=0, keepdims=True) + LN_EPS)
        o_ref[:, h * HEAD_W:(h + 1) * HEAD_W] = (
            o.T * sg_ref[...] * (1.0 - lambda_init)).astype(BF16)


def _diff_attention(qt, k, vt, lq1, lk1, lq2, lk2, sg, batch, seq, lambda_init):
    n = k.shape[0]
    steps = seq // TQ
    qt_spec = pl.BlockSpec((None, HALF, TQ), lambda b, i: (b * steps + i, 0, 0))
    k_spec = pl.BlockSpec((seq, HALF), lambda b, i: (b, 0))
    vt_spec = pl.BlockSpec((steps, HALF, TQ), lambda b, i: (b, 0, 0))
    o_spec = pl.BlockSpec((TQ, HALF), lambda b, i: (b * steps + i, 0))
    return pl.pallas_call(
        functools.partial(_attn_kernel, lambda_init=lambda_init),
        grid=(batch, steps),
        in_specs=[qt_spec, k_spec, vt_spec, _const_spec(lq1.shape), _const_spec(lk1.shape),
                  _const_spec(lq2.shape), _const_spec(lk2.shape), _const_spec(sg.shape)],
        out_specs=o_spec,
        out_shape=jax.ShapeDtypeStruct((n, HALF), BF16),
        scratch_shapes=[pltpu.VMEM((C_HEADS, HEAD_W, 2 * TQ), BF16),
                        pltpu.VMEM((C_HEADS, 1, 2 * TQ), F32),
                        pltpu.VMEM((C_HEADS, HEAD_W + ONES_ROWS, 2 * TQ), F32)],
        compiler_params=pltpu.CompilerParams(
            dimension_semantics=("arbitrary", "arbitrary"), vmem_limit_bytes=VMEM_LIMIT),
        name="diff_attention",
    )(qt, k, vt, lq1, lk1, lq2, lk2, sg)


def kernel(x, even_w_in, even_conv_a_w, even_conv_b_w, even_conv_b_bias, even_conv_ln_g, even_conv_ln_b, even_w_out, odd_w_in, odd_lambda_q1, odd_lambda_k1, odd_lambda_q2, odd_lambda_k2, odd_subln_g, odd_gmlp_ln_g, odd_gmlp_ln_b, odd_spatial_w, odd_spatial_b, odd_w_out, mix_ln_g, mix_ln_b, ffn_w_gate, ffn_w_up, ffn_w_down, ffn_ln_g, ffn_ln_b):
    batch, seq, d = x.shape
    assert d == D_MODEL and seq % TILE == 0 and seq % TQ == 0
    xf = x.reshape(batch * seq, d)
    row = lambda a: a.reshape(1, -1)
    bf = lambda a: a.astype(BF16)

    def post(layer, ya, yb, xres, w_out):
        return _post_block(ya, yb, xres, bf(w_out), row(mix_ln_g[layer]), row(mix_ln_b[layer]),
                           bf(ffn_w_gate[layer]), bf(ffn_w_up[layer]), bf(ffn_w_down[layer]),
                           row(ffn_ln_g[layer]), row(ffn_ln_b[layer]))

    ya, yb = _even_mixer(xf, bf(even_w_in[0]), even_conv_a_w[0], even_conv_b_w[0],
                         row(even_conv_b_bias[0]), row(even_conv_ln_g[0]), row(even_conv_ln_b[0]),
                         batch, seq)
    xf = post(0, ya, yb, xf, even_w_out[0])

    lambda_init = 0.8 - 0.6 * math.exp(-0.3 * 1)
    sb = jnp.broadcast_to(odd_spatial_b[0].T[:, :, None], (CHUNK, D_GROUPS, CHUNK)).reshape(CHUNK, HALF)
    w_in = bf(odd_w_in[0])
    wqv_t = jnp.concatenate([w_in[:, 0:HALF], w_in[:, 2 * HALF:3 * HALF]], axis=1).T
    qt, k, vt, yd = _odd_proj(xf, w_in, wqv_t, row(odd_gmlp_ln_g[0]), row(odd_gmlp_ln_b[0]),
                              odd_spatial_w[0], sb)
    yc = _diff_attention(qt, k, vt, row(odd_lambda_q1[0]), row(odd_lambda_k1[0]),
                         row(odd_lambda_q2[0]), row(odd_lambda_k2[0]), row(odd_subln_g[0]),
                         batch, seq, lambda_init)
    xf = post(1, yc, yd, xf, odd_w_out[0])
    return xf.reshape(batch, seq, d)
```

```python
import functools
import math

import jax
import jax.numpy as jnp
from jax import lax
from jax.experimental import pallas as pl
from jax.experimental.pallas import tpu as pltpu

F32 = jnp.float32
BF16 = jnp.bfloat16

D_MODEL = 1024
HALF = D_MODEL // 2
A_CONV = 3
B_CONV = 31
C_HEAD_DIM = 64
C_HEADS = 4
HEAD_W = 2 * C_HEAD_DIM
CHUNK = 128
D_GROUPS = 4
D_FF = 2816
DEPTH = 2
ALPHA = (2 * DEPTH) ** 0.25
LN_EPS = 1e-5
LOG2E = 1.4426950408889634

SUBLANES = 8
HALO_A = 8
HALO_B = 32
CONV_ROWS = 32

TILE = 512
FF_CHUNK = 256
TQ = 512
TK = 256
VMEM_LIMIT = 56 * 1024 * 1024


def _layer_norm(x, g, b):
    mu = jnp.mean(x, axis=-1, keepdims=True)
    xc = x - mu
    var = jnp.mean(xc * xc, axis=-1, keepdims=True)
    return xc * lax.rsqrt(var + LN_EPS) * g + b


def _sigmoid(x):
    return 1.0 / (1.0 + jnp.exp(-x))


def _dot(a, b):
    return jnp.dot(a, b, preferred_element_type=F32)


def _const_spec(shape):
    zeros = (0,) * len(shape)
    return pl.BlockSpec(shape, lambda *_: zeros, pipeline_mode=pl.Buffered(1))


def _causal_conv(ext_ref, shifted_ref, w_rows, first_row, rows, out_fn):
    def window(offset):
        r = offset % SUBLANES
        if shifted_ref is None or r == 0:
            return ext_ref[pl.ds(offset, CONV_ROWS), :]
        return shifted_ref[r - 1, pl.ds(offset - r, CONV_ROWS), :]

    for r0 in range(0, rows, CONV_ROWS):
        acc = w_rows[0] * window(first_row + r0)
        for k in range(1, len(w_rows)):
            acc = acc + w_rows[k] * window(first_row + r0 + k)
        out_fn(r0, acc)


def _replicated_taps(w_ref, wrep_ref, taps):
    for k in range(taps):
        wrep_ref[k] = jnp.broadcast_to(w_ref[k:k + 1, :], (SUBLANES, w_ref.shape[1]))
    return [jnp.concatenate([wrep_ref[k]] * (CONV_ROWS // SUBLANES), axis=0) for k in range(taps)]


def _even_mixer_kernel(x_ref, w_in_ref, wa_ref, wb_ref, bias_ref, g_ref, b_ref,
                       ya_ref, yb_ref, aext, zext, zshift, warep, wbrep, gate_scr, conv_scr):
    rows = x_ref.shape[0]

    @pl.when(pl.program_id(1) == 0)
    def _():
        aext[0:HALO_A, :] = jnp.zeros((HALO_A, HALF), F32)
        zext[0:HALO_B, :] = jnp.zeros((HALO_B, HALF), F32)

    xb = x_ref[...].astype(BF16)

    def proj(c):
        return _dot(xb, w_in_ref[:, c * HALF:(c + 1) * HALF])

    zext[HALO_B:HALO_B + rows, :] = proj(3) * _sigmoid(proj(4))
    gate_scr[...] = proj(0)
    aext[HALO_A:HALO_A + rows, :] = proj(1) * proj(2)

    for r in range(1, SUBLANES):
        zshift[r - 1] = zext[pl.ds(r, rows + HALO_B - SUBLANES), :]

    def store_b(r0, acc):
        conv_scr[pl.ds(r0, CONV_ROWS), :] = acc

    _causal_conv(zext, zshift, _replicated_taps(wb_ref, wbrep, B_CONV),
                 HALO_B - (B_CONV - 1), rows, store_b)
    zext[0:HALO_B, :] = zext[rows:rows + HALO_B, :]

    zn = _layer_norm(conv_scr[...] + bias_ref[...], g_ref[...], b_ref[...])
    yb_ref[...] = (zn * _sigmoid(zn)).astype(BF16)

    def store_a(r0, acc):
        ya_ref[pl.ds(r0, CONV_ROWS), :] = (gate_scr[pl.ds(r0, CONV_ROWS), :] * acc).astype(BF16)

    _causal_conv(aext, None, _replicated_taps(wa_ref, warep, A_CONV),
                 HALO_A - (A_CONV - 1), rows, store_a)
    aext[0:HALO_A, :] = aext[rows:rows + HALO_A, :]


def _even_mixer(x, w_in, wa, wb, bias, g, b, batch, seq):
    n = x.shape[0]
    steps = seq // TILE
    row_spec = lambda w: pl.BlockSpec((TILE, w), lambda bi, si: (bi * steps + si, 0))
    return pl.pallas_call(
        _even_mixer_kernel,
        grid=(batch, steps),
        in_specs=[row_spec(D_MODEL), _const_spec(w_in.shape), _const_spec(wa.shape),
                  _const_spec(wb.shape), _const_spec(bias.shape), _const_spec(g.shape),
                  _const_spec(b.shape)],
        out_specs=[row_spec(HALF), row_spec(HALF)],
        out_shape=[jax.ShapeDtypeStruct((n, HALF), BF16)] * 2,
        scratch_shapes=[pltpu.VMEM((TILE + HALO_A, HALF), F32),
                        pltpu.VMEM((TILE + HALO_B, HALF), F32),
                        pltpu.VMEM((SUBLANES - 1, TILE + HALO_B - SUBLANES, HALF), F32),
                        pltpu.VMEM((A_CONV, SUBLANES, HALF), F32),
                        pltpu.VMEM((B_CONV, SUBLANES, HALF), F32),
                        pltpu.VMEM((TILE, HALF), F32),
                        pltpu.VMEM((TILE, HALF), F32)],
        compiler_params=pltpu.CompilerParams(
            dimension_semantics=("arbitrary", "arbitrary"), vmem_limit_bytes=VMEM_LIMIT),
        name="even_mixer",
    )(x, w_in, wa, wb, bias, g, b)


def _post_kernel(ya_ref, yb_ref, x_ref, wout_ref, mg_ref, mb_ref, wg_ref, wu_ref, wd_ref,
                 fg_ref, fb_ref, o_ref, h_scr):
    m = _dot(ya_ref[...], wout_ref[0:HALF, :]) + _dot(yb_ref[...], wout_ref[HALF:D_MODEL, :])
    x1 = _layer_norm(ALPHA * x_ref[...] + m, mg_ref[...], mb_ref[...])
    xb = x1.astype(BF16)
    for c in range(0, D_FF, FF_CHUNK):
        gate = _dot(xb, wg_ref[:, c:c + FF_CHUNK])
        up = _dot(xb, wu_ref[:, c:c + FF_CHUNK])
        h_scr[:, c:c + FF_CHUNK] = (gate * _sigmoid(gate) * up).astype(BF16)
    f = _dot(h_scr[...], wd_ref[...])
    o_ref[...] = _layer_norm(ALPHA * x1 + f, fg_ref[...], fb_ref[...])


def _post_block(ya, yb, x, wout, mg, mb, wg, wu, wd, fg, fb):
    n = x.shape[0]
    row_spec = lambda w: pl.BlockSpec((TILE, w), lambda i: (i, 0))
    return pl.pallas_call(
        _post_kernel,
        grid=(n // TILE,),
        in_specs=[row_spec(HALF), row_spec(HALF), row_spec(D_MODEL), _const_spec(wout.shape),
                  _const_spec(mg.shape), _const_spec(mb.shape), _const_spec(wg.shape),
                  _const_spec(wu.shape), _const_spec(wd.shape), _const_spec(fg.shape),
                  _const_spec(fb.shape)],
        out_specs=row_spec(D_MODEL),
        out_shape=jax.ShapeDtypeStruct((n, D_MODEL), F32),
        scratch_shapes=[pltpu.VMEM((TILE, D_FF), BF16)],
        compiler_params=pltpu.CompilerParams(
            dimension_semantics=("arbitrary",), vmem_limit_bytes=VMEM_LIMIT),
        name="post_block",
    )(ya, yb, x, wout, mg, mb, wg, wu, wd, fg, fb)


def _gelu_tanh(x):
    return 0.5 * x * (1.0 + jnp.tanh(math.sqrt(2.0 / math.pi) * (x + 0.044715 * (x * x * x))))


def _dot_nt(a, b):
    return lax.dot_general(a, b, (((1,), (1,)), ((), ())), preferred_element_type=F32)


def _odd_proj_kernel(x_ref, w_in_ref, wqv_t_ref, lg_ref, lb_ref, sw_ref, sb_ref,
                     qt_ref, k_ref, vt_ref, yd_ref):
    rows = x_ref.shape[0]
    xb = x_ref[...].astype(BF16)

    def proj(c):
        return _dot(xb, w_in_ref[:, c * HALF:(c + 1) * HALF])

    q_t = (_dot_nt(wqv_t_ref[0:HALF, :], xb) * (C_HEAD_DIM ** -0.5 * LOG2E)).astype(BF16)
    v_t = _dot_nt(wqv_t_ref[HALF:2 * HALF, :], xb).astype(BF16)
    for t in range(rows // TQ):
        qt_ref[t] = q_t[:, t * TQ:(t + 1) * TQ]
    for t in range(rows // TK):
        vt_ref[t] = v_t[:, t * TK:(t + 1) * TK]
    k_ref[...] = proj(1).astype(BF16)

    u = _gelu_tanh(proj(3))
    vg = _gelu_tanh(proj(4))
    row = lax.broadcasted_iota(jnp.int32, (CHUNK, CHUNK), 0)
    col = lax.broadcasted_iota(jnp.int32, (CHUNK, CHUNK), 1)
    for g in range(D_GROUPS):
        lanes = slice(g * CHUNK, (g + 1) * CHUNK)
        vn = _layer_norm(vg[:, lanes], lg_ref[:, lanes], lb_ref[:, lanes]).astype(BF16)
        w_causal = jnp.where(col <= row, sw_ref[g], 0.0).astype(BF16)
        for r0 in range(0, rows, CHUNK):
            sp = _dot(w_causal, vn[r0:r0 + CHUNK, :]) + sb_ref[:, lanes]
            yd_ref[r0:r0 + CHUNK, lanes] = (u[r0:r0 + CHUNK, lanes] * sp).astype(BF16)


def _odd_proj(x, w_in, wqv_t, lg, lb, sw, sb):
    n = x.shape[0]
    row_spec = lambda w: pl.BlockSpec((TILE, w), lambda i: (i, 0))
    t_spec = lambda blk: pl.BlockSpec((TILE // blk, HALF, blk), lambda i: (i, 0, 0))
    t_shape = lambda blk: jax.ShapeDtypeStruct((n // blk, HALF, blk), BF16)
    r_shape = jax.ShapeDtypeStruct((n, HALF), BF16)
    return pl.pallas_call(
        _odd_proj_kernel,
        grid=(n // TILE,),
        in_specs=[row_spec(D_MODEL), _const_spec(w_in.shape), _const_spec(wqv_t.shape),
                  _const_spec(lg.shape), _const_spec(lb.shape), _const_spec(sw.shape),
                  _const_spec(sb.shape)],
        out_specs=[t_spec(TQ), row_spec(HALF), t_spec(TK), row_spec(HALF)],
        out_shape=[t_shape(TQ), r_shape, t_shape(TK), r_shape],
        compiler_params=pltpu.CompilerParams(
            dimension_semantics=("arbitrary",), vmem_limit_bytes=VMEM_LIMIT),
        name="odd_proj",
    )(x, w_in, wqv_t, lg, lb, sw, sb)


ONES_ROWS = 16


def _attn_kernel(qt_ref, k_ref, vt_ref, lq1_ref, lk1_ref, lq2_ref, lk2_ref, sg_ref, o_ref,
                 qs_scr, s_scr, cmax_scr, p_scr, alpha_scr, m_scr, acc_scr, *, lambda_init):
    qi = pl.program_id(1)
    heads = range(C_HEADS)
    head_cols = lambda h: slice(h * HEAD_W, (h + 1) * HEAD_W)

    chan = lax.broadcasted_iota(jnp.int32, (HEAD_W, TQ), 0)
    zero = jnp.zeros((HEAD_W, TQ), BF16)
    for h in heads:
        q_t = qt_ref[head_cols(h), :]
        qs_scr[h, :, 0:TQ] = jnp.where(chan < C_HEAD_DIM, q_t, zero)
        qs_scr[h, :, TQ:2 * TQ] = jnp.where(chan >= C_HEAD_DIM, q_t, zero)
    m_scr[...] = jnp.full(m_scr.shape, -jnp.inf, F32)
    acc_scr[...] = jnp.zeros(acc_scr.shape, F32)
    p_scr[1] = jnp.zeros(p_scr.shape[1:], BF16)
    alpha_scr[1] = jnp.ones(alpha_scr.shape[1:], F32)
    ones = jnp.ones((ONES_ROWS, TK), BF16)

    def scores(t, slot):
        rows = pl.ds(pl.multiple_of(t * TK, TK), TK)
        for h in heads:
            s = _dot(k_ref[rows, head_cols(h)], qs_scr[h])
            s_scr[slot, h] = s
            cmax_scr[slot, h] = jnp.max(s, axis=0, keepdims=True)

    def softmax(slot, keep):
        for h in heads:
            s = s_scr[slot, h]
            if keep is None:
                cmax = cmax_scr[slot, h]
            else:
                s = jnp.where(keep, s, -jnp.inf)
                cmax = jnp.max(s, axis=0, keepdims=True)
            m_old = m_scr[h]
            m_new = jnp.maximum(m_old, cmax)
            p_scr[slot, h] = jnp.exp2(s - m_new).astype(BF16)
            alpha_scr[slot, h] = jnp.exp2(m_old - m_new)
            m_scr[h] = m_new

    def fold(t, slot):
        for h in heads:
            v_ext = jnp.concatenate([vt_ref[t, head_cols(h), :], ones], axis=0)
            acc_scr[h] = alpha_scr[slot, h] * acc_scr[h] + _dot(v_ext, p_scr[slot, h])

    scores(0, 0)

    def pair(u, carry):
        t = 2 * u
        fold(jnp.maximum(t - 1, 0), 1)
        scores(t + 1, 1)
        softmax(0, None)
        fold(t, 0)
        scores(t + 2, 0)
        softmax(1, None)
        return carry

    lax.fori_loop(0, qi, pair, 0)

    t = 2 * qi
    key = lax.broadcasted_iota(jnp.int32, (TK, TQ), 0)
    qry = lax.broadcasted_iota(jnp.int32, (TK, TQ), 1)
    keep_lo = jnp.concatenate([key <= qry] * 2, axis=1)
    keep_hi = jnp.concatenate([key + TK <= qry] * 2, axis=1)
    fold(jnp.maximum(t - 1, 0), 1)
    scores(t + 1, 1)
    softmax(0, keep_lo)
    fold(t, 0)
    softmax(1, keep_hi)
    fold(t + 1, 1)

    lam = (jnp.exp(jnp.sum(lq1_ref[...] * lk1_ref[...], axis=-1, keepdims=True))
           - jnp.exp(jnp.sum(lq2_ref[...] * lk2_ref[...], axis=-1, keepdims=True))
           + lambda_init)
    for h in heads:
        o = acc_scr[h, 0:HEAD_W, :] / acc_scr[h, HEAD_W:HEAD_W + 1, :]
        o = o[:, 0:TQ] - lam * o[:, TQ:2 * TQ]
        o = o * lax.rsqrt(jnp.mean(o * o, axis=0, keepdims=True) + LN_EPS)
        o_ref[:, head_cols(h)] = (o.T * sg_ref[...] * (1.0 - lambda_init)).astype(BF16)


def _diff_attention(qt, k, vt, lq1, lk1, lq2, lk2, sg, batch, seq, lambda_init):
    n = k.shape[0]
    steps = seq // TQ
    qt_spec = pl.BlockSpec((None, HALF, TQ), lambda b, i: (b * steps + i, 0, 0))
    k_spec = pl.BlockSpec((seq, HALF), lambda b, i: (b, 0))
    vt_spec = pl.BlockSpec((seq // TK, HALF, TK), lambda b, i: (b, 0, 0))
    o_spec = pl.BlockSpec((TQ, HALF), lambda b, i: (b * steps + i, 0))
    return pl.pallas_call(
        functools.partial(_attn_kernel, lambda_init=lambda_init),
        grid=(batch, steps),
        in_specs=[qt_spec, k_spec, vt_spec, _const_spec(lq1.shape), _const_spec(lk1.shape),
                  _const_spec(lq2.shape), _const_spec(lk2.shape), _const_spec(sg.shape)],
        out_specs=o_spec,
        out_shape=jax.ShapeDtypeStruct((n, HALF), BF16),
        scratch_shapes=[pltpu.VMEM((C_HEADS, HEAD_W, 2 * TQ), BF16),
                        pltpu.VMEM((2, C_HEADS, TK, 2 * TQ), F32),
                        pltpu.VMEM((2, C_HEADS, 1, 2 * TQ), F32),
                        pltpu.VMEM((2, C_HEADS, TK, 2 * TQ), BF16),
                        pltpu.VMEM((2, C_HEADS, 1, 2 * TQ), F32),
                        pltpu.VMEM((C_HEADS, 1, 2 * TQ), F32),
                        pltpu.VMEM((C_HEADS, HEAD_W + ONES_ROWS, 2 * TQ), F32)],
        compiler_params=pltpu.CompilerParams(
            dimension_semantics=("arbitrary", "arbitrary"), vmem_limit_bytes=VMEM_LIMIT),
        name="diff_attention",
    )(qt, k, vt, lq1, lk1, lq2, lk2, sg)


def kernel(x, even_w_in, even_conv_a_w, even_conv_b_w, even_conv_b_bias, even_conv_ln_g, even_conv_ln_b, even_w_out, odd_w_in, odd_lambda_q1, odd_lambda_k1, odd_lambda_q2, odd_lambda_k2, odd_subln_g, odd_gmlp_ln_g, odd_gmlp_ln_b, odd_spatial_w, odd_spatial_b, odd_w_out, mix_ln_g, mix_ln_b, ffn_w_gate, ffn_w_up, ffn_w_down, ffn_ln_g, ffn_ln_b):
    batch, seq, d = x.shape
    assert d == D_MODEL and seq % TILE == 0 and TILE % TQ == 0 and TQ == 2 * TK
    xf = x.reshape(batch * seq, d)
    row = lambda a: a.reshape(1, -1)
    bf = lambda a: a.astype(BF16)

    def post(layer, ya, yb, xres, w_out):
        return _post_block(ya, yb, xres, bf(w_out), row(mix_ln_g[layer]), row(mix_ln_b[layer]),
                           bf(ffn_w_gate[layer]), bf(ffn_w_up[layer]), bf(ffn_w_down[layer]),
                           row(ffn_ln_g[layer]), row(ffn_ln_b[layer]))

    ya, yb = _even_mixer(xf, bf(even_w_in[0]), even_conv_a_w[0], even_conv_b_w[0],
                         row(even_conv_b_bias[0]), row(even_conv_ln_g[0]), row(even_conv_ln_b[0]),
                         batch, seq)
    xf = post(0, ya, yb, xf, even_w_out[0])

    lambda_init = 0.8 - 0.6 * math.exp(-0.3 * 1)
    sb = jnp.broadcast_to(odd_spatial_b[0].T[:, :, None], (CHUNK, D_GROUPS, CHUNK)).reshape(CHUNK, HALF)
    w_in = bf(odd_w_in[0])
    wqv_t = jnp.concatenate([w_in[:, 0:HALF], w_in[:, 2 * HALF:3 * HALF]], axis=1).T
    qt, k, vt, yd = _odd_proj(xf, w_in, wqv_t, row(odd_gmlp_ln_g[0]), row(odd_gmlp_ln_b[0]),
                              odd_spatial_w[0], sb)
    yc = _diff_attention(qt, k, vt, row(odd_lambda_q1[0]), row(odd_lambda_k1[0]),
                         row(odd_lambda_q2[0]), row(odd_lambda_k2[0]), row(odd_subln_g[0]),
                         batch, seq, lambda_init)
    xf = post(1, yc, yd, xf, odd_w_out[0])
    return xf.reshape(batch, seq, d)
```

```python
import functools
import math

import jax
import jax.numpy as jnp
from jax import lax
from jax.experimental import pallas as pl
from jax.experimental.pallas import tpu as pltpu

F32 = jnp.float32
BF16 = jnp.bfloat16

D_MODEL = 1024
HALF = D_MODEL // 2
A_CONV = 3
B_CONV = 31
C_HEAD_DIM = 64
C_HEADS = 4
HEAD_W = 2 * C_HEAD_DIM
CHUNK = 128
D_GROUPS = 4
D_FF = 2816
DEPTH = 2
ALPHA = (2 * DEPTH) ** 0.25
LN_EPS = 1e-5
LOG2E = 1.4426950408889634

SUBLANES = 8
HALO_A = 8
HALO_B = 32
CONV_ROWS = 32

TILE = 512
FF_CHUNK = 256
TQ = 512
TK = 256
VMEM_LIMIT = 56 * 1024 * 1024


def _layer_norm(x, g, b):
    mu = jnp.mean(x, axis=-1, keepdims=True)
    xc = x - mu
    var = jnp.mean(xc * xc, axis=-1, keepdims=True)
    return xc * lax.rsqrt(var + LN_EPS) * g + b


def _sigmoid(x):
    return 1.0 / (1.0 + jnp.exp(-x))


def _dot(a, b):
    return jnp.dot(a, b, preferred_element_type=F32)


def _const_spec(shape):
    zeros = (0,) * len(shape)
    return pl.BlockSpec(shape, lambda *_: zeros, pipeline_mode=pl.Buffered(1))


def _causal_conv(ext_ref, shifted_ref, w_rows, first_row, rows, out_fn):
    def window(offset):
        r = offset % SUBLANES
        if shifted_ref is None or r == 0:
            return ext_ref[pl.ds(offset, CONV_ROWS), :]
        return shifted_ref[r - 1, pl.ds(offset - r, CONV_ROWS), :]

    for r0 in range(0, rows, CONV_ROWS):
        acc = w_rows[0] * window(first_row + r0)
        for k in range(1, len(w_rows)):
            acc = acc + w_rows[k] * window(first_row + r0 + k)
        out_fn(r0, acc)


def _replicated_taps(w_ref, wrep_ref, taps):
    for k in range(taps):
        wrep_ref[k] = jnp.broadcast_to(w_ref[k:k + 1, :], (SUBLANES, w_ref.shape[1]))
    return [jnp.concatenate([wrep_ref[k]] * (CONV_ROWS // SUBLANES), axis=0) for k in range(taps)]


def _even_mixer_kernel(x_ref, w_in_ref, wa_ref, wb_ref, bias_ref, g_ref, b_ref,
                       ya_ref, yb_ref, aext, zext, zshift, warep, wbrep, gate_scr, conv_scr):
    rows = x_ref.shape[0]

    @pl.when(pl.program_id(1) == 0)
    def _():
        aext[0:HALO_A, :] = jnp.zeros((HALO_A, HALF), F32)
        zext[0:HALO_B, :] = jnp.zeros((HALO_B, HALF), F32)

    xb = x_ref[...].astype(BF16)

    def proj(c):
        return _dot(xb, w_in_ref[:, c * HALF:(c + 1) * HALF])

    zext[HALO_B:HALO_B + rows, :] = proj(3) * _sigmoid(proj(4))
    gate_scr[...] = proj(0)
    aext[HALO_A:HALO_A + rows, :] = proj(1) * proj(2)

    for r in range(1, SUBLANES):
        zshift[r - 1] = zext[pl.ds(r, rows + HALO_B - SUBLANES), :]

    def store_b(r0, acc):
        conv_scr[pl.ds(r0, CONV_ROWS), :] = acc

    _causal_conv(zext, zshift, _replicated_taps(wb_ref, wbrep, B_CONV),
                 HALO_B - (B_CONV - 1), rows, store_b)
    zext[0:HALO_B, :] = zext[rows:rows + HALO_B, :]

    zn = _layer_norm(conv_scr[...] + bias_ref[...], g_ref[...], b_ref[...])
    yb_ref[...] = (zn * _sigmoid(zn)).astype(BF16)

    def store_a(r0, acc):
        ya_ref[pl.ds(r0, CONV_ROWS), :] = (gate_scr[pl.ds(r0, CONV_ROWS), :] * acc).astype(BF16)

    _causal_conv(aext, None, _replicated_taps(wa_ref, warep, A_CONV),
                 HALO_A - (A_CONV - 1), rows, store_a)
    aext[0:HALO_A, :] = aext[rows:rows + HALO_A, :]


def _even_mixer(x, w_in, wa, wb, bias, g, b, batch, seq):
    n = x.shape[0]
    steps = seq // TILE
    row_spec = lambda w: pl.BlockSpec((TILE, w), lambda bi, si: (bi * steps + si, 0))
    return pl.pallas_call(
        _even_mixer_kernel,
        grid=(batch, steps),
        in_specs=[row_spec(D_MODEL), _const_spec(w_in.shape), _const_spec(wa.shape),
                  _const_spec(wb.shape), _const_spec(bias.shape), _const_spec(g.shape),
                  _const_spec(b.shape)],
        out_specs=[row_spec(HALF), row_spec(HALF)],
        out_shape=[jax.ShapeDtypeStruct((n, HALF), BF16)] * 2,
        scratch_shapes=[pltpu.VMEM((TILE + HALO_A, HALF), F32),
                        pltpu.VMEM((TILE + HALO_B, HALF), F32),
                        pltpu.VMEM((SUBLANES - 1, TILE + HALO_B - SUBLANES, HALF), F32),
                        pltpu.VMEM((A_CONV, SUBLANES, HALF), F32),
                        pltpu.VMEM((B_CONV, SUBLANES, HALF), F32),
                        pltpu.VMEM((TILE, HALF), F32),
                        pltpu.VMEM((TILE, HALF), F32)],
        compiler_params=pltpu.CompilerParams(
            dimension_semantics=("arbitrary", "arbitrary"), vmem_limit_bytes=VMEM_LIMIT),
        name="even_mixer",
    )(x, w_in, wa, wb, bias, g, b)


def _post_kernel(ya_ref, yb_ref, x_ref, wout_ref, mg_ref, mb_ref, wg_ref, wu_ref, wd_ref,
                 fg_ref, fb_ref, o_ref, h_scr):
    m = _dot(ya_ref[...], wout_ref[0:HALF, :]) + _dot(yb_ref[...], wout_ref[HALF:D_MODEL, :])
    x1 = _layer_norm(ALPHA * x_ref[...] + m, mg_ref[...], mb_ref[...])
    xb = x1.astype(BF16)
    for c in range(0, D_FF, FF_CHUNK):
        gate = _dot(xb, wg_ref[:, c:c + FF_CHUNK])
        up = _dot(xb, wu_ref[:, c:c + FF_CHUNK])
        h_scr[:, c:c + FF_CHUNK] = (gate * _sigmoid(gate) * up).astype(BF16)
    f = _dot(h_scr[...], wd_ref[...])
    o_ref[...] = _layer_norm(ALPHA * x1 + f, fg_ref[...], fb_ref[...])


def _post_block(ya, yb, x, wout, mg, mb, wg, wu, wd, fg, fb):
    n = x.shape[0]
    row_spec = lambda w: pl.BlockSpec((TILE, w), lambda i: (i, 0))
    return pl.pallas_call(
        _post_kernel,
        grid=(n // TILE,),
        in_specs=[row_spec(HALF), row_spec(HALF), row_spec(D_MODEL), _const_spec(wout.shape),
                  _const_spec(mg.shape), _const_spec(mb.shape), _const_spec(wg.shape),
                  _const_spec(wu.shape), _const_spec(wd.shape), _const_spec(fg.shape),
                  _const_spec(fb.shape)],
        out_specs=row_spec(D_MODEL),
        out_shape=jax.ShapeDtypeStruct((n, D_MODEL), F32),
        scratch_shapes=[pltpu.VMEM((TILE, D_FF), BF16)],
        compiler_params=pltpu.CompilerParams(
            dimension_semantics=("arbitrary",), vmem_limit_bytes=VMEM_LIMIT),
        name="post_block",
    )(ya, yb, x, wout, mg, mb, wg, wu, wd, fg, fb)


def _gelu_tanh(x):
    return 0.5 * x * (1.0 + jnp.tanh(math.sqrt(2.0 / math.pi) * (x + 0.044715 * (x * x * x))))


def _dot_nt(a, b):
    return lax.dot_general(a, b, (((1,), (1,)), ((), ())), preferred_element_type=F32)


def _odd_proj_kernel(x_ref, w_in_ref, wqv_t_ref, lg_ref, lb_ref, sw_ref, sb_ref,
                     qt_ref, k_ref, vt_ref, yd_ref):
    rows = x_ref.shape[0]
    xb = x_ref[...].astype(BF16)

    def proj(c):
        return _dot(xb, w_in_ref[:, c * HALF:(c + 1) * HALF])

    q_t = (_dot_nt(wqv_t_ref[0:HALF, :], xb) * (C_HEAD_DIM ** -0.5 * LOG2E)).astype(BF16)
    v_t = _dot_nt(wqv_t_ref[HALF:2 * HALF, :], xb).astype(BF16)
    for t in range(rows // TQ):
        qt_ref[t] = q_t[:, t * TQ:(t + 1) * TQ]
    for t in range(rows // TK):
        vt_ref[t] = v_t[:, t * TK:(t + 1) * TK]
    k_ref[...] = proj(1).astype(BF16)

    u = _gelu_tanh(proj(3))
    vg = _gelu_tanh(proj(4))
    row = lax.broadcasted_iota(jnp.int32, (CHUNK, CHUNK), 0)
    col = lax.broadcasted_iota(jnp.int32, (CHUNK, CHUNK), 1)
    for g in range(D_GROUPS):
        lanes = slice(g * CHUNK, (g + 1) * CHUNK)
        vn = _layer_norm(vg[:, lanes], lg_ref[:, lanes], lb_ref[:, lanes]).astype(BF16)
        w_causal = jnp.where(col <= row, sw_ref[g], 0.0).astype(BF16)
        for r0 in range(0, rows, CHUNK):
            sp = _dot(w_causal, vn[r0:r0 + CHUNK, :]) + sb_ref[:, lanes]
            yd_ref[r0:r0 + CHUNK, lanes] = (u[r0:r0 + CHUNK, lanes] * sp).astype(BF16)


def _odd_proj(x, w_in, wqv_t, lg, lb, sw, sb):
    n = x.shape[0]
    row_spec = lambda w: pl.BlockSpec((TILE, w), lambda i: (i, 0))
    t_spec = lambda blk: pl.BlockSpec((TILE // blk, HALF, blk), lambda i: (i, 0, 0))
    t_shape = lambda blk: jax.ShapeDtypeStruct((n // blk, HALF, blk), BF16)
    r_shape = jax.ShapeDtypeStruct((n, HALF), BF16)
    return pl.pallas_call(
        _odd_proj_kernel,
        grid=(n // TILE,),
        in_specs=[row_spec(D_MODEL), _const_spec(w_in.shape), _const_spec(wqv_t.shape),
                  _const_spec(lg.shape), _const_spec(lb.shape), _const_spec(sw.shape),
                  _const_spec(sb.shape)],
        out_specs=[t_spec(TQ), row_spec(HALF), t_spec(TK), row_spec(HALF)],
        out_shape=[t_shape(TQ), r_shape, t_shape(TK), r_shape],
        compiler_params=pltpu.CompilerParams(
            dimension_semantics=("arbitrary",), vmem_limit_bytes=VMEM_LIMIT),
        name="odd_proj",
    )(x, w_in, wqv_t, lg, lb, sw, sb)


ONES_ROWS = 16


def _attn_kernel(qt_ref, k_ref, vt_ref, lq1_ref, lk1_ref, lq2_ref, lk2_ref, sg_ref, o_ref,
                 qs_scr, s_scr, cmax_scr, m_scr, acc_scr, *, lambda_init):
    qi = pl.program_id(1)
    heads = range(C_HEADS)
    head_cols = lambda h: slice(h * HEAD_W, (h + 1) * HEAD_W)

    chan = lax.broadcasted_iota(jnp.int32, (HEAD_W, TQ), 0)
    zero = jnp.zeros((HEAD_W, TQ), BF16)
    for h in heads:
        q_t = qt_ref[head_cols(h), :]
        qs_scr[h, :, 0:TQ] = jnp.where(chan < C_HEAD_DIM, q_t, zero)
        qs_scr[h, :, TQ:2 * TQ] = jnp.where(chan >= C_HEAD_DIM, q_t, zero)
    m_scr[...] = jnp.full(m_scr.shape, -jnp.inf, F32)
    acc_scr[...] = jnp.zeros(acc_scr.shape, F32)
    ones = jnp.ones((ONES_ROWS, TK), BF16)

    def scores(t, slot, h):
        rows = pl.ds(pl.multiple_of(t * TK, TK), TK)
        s = _dot(k_ref[rows, head_cols(h)], qs_scr[h])
        s_scr[slot, h] = s
        cmax_scr[slot, h] = jnp.max(s, axis=0, keepdims=True)

    def fold(t, slot, h, keep):
        s = s_scr[slot, h]
        if keep is None:
            cmax = cmax_scr[slot, h]
        else:
            s = jnp.where(keep, s, -jnp.inf)
            cmax = jnp.max(s, axis=0, keepdims=True)
        m_old = m_scr[h]
        m_new = jnp.maximum(m_old, cmax)
        p = jnp.exp2(s - m_new).astype(BF16)
        v_ext = jnp.concatenate([vt_ref[t, head_cols(h), :], ones], axis=0)
        acc_scr[h] = jnp.exp2(m_old - m_new) * acc_scr[h] + _dot(v_ext, p)
        m_scr[h] = m_new

    def step(t, slot, keep=None, prefetch=True):
        for h in range(C_HEADS + 1):
            if prefetch and h < C_HEADS:
                scores(t + 1, 1 - slot, h)
            if h > 0:
                fold(t, slot, h - 1, keep)

    for h in heads:
        scores(0, 0, h)

    def pair(u, carry):
        step(2 * u, 0)
        step(2 * u + 1, 1)
        return carry

    lax.fori_loop(0, qi, pair, 0)

    key = lax.broadcasted_iota(jnp.int32, (TK, TQ), 0)
    qry = lax.broadcasted_iota(jnp.int32, (TK, TQ), 1)
    step(2 * qi, 0, keep=jnp.concatenate([key <= qry] * 2, axis=1))
    step(2 * qi + 1, 1, keep=jnp.concatenate([key + TK <= qry] * 2, axis=1), prefetch=False)

    lam = (jnp.exp(jnp.sum(lq1_ref[...] * lk1_ref[...], axis=-1, keepdims=True))
           - jnp.exp(jnp.sum(lq2_ref[...] * lk2_ref[...], axis=-1, keepdims=True))
           + lambda_init)
    for h in heads:
        o = acc_scr[h, 0:HEAD_W, :] / acc_scr[h, HEAD_W:HEAD_W + 1, :]
        o = o[:, 0:TQ] - lam * o[:, TQ:2 * TQ]
        o = o * lax.rsqrt(jnp.mean(o * o, axis=0, keepdims=True) + LN_EPS)
        o_ref[:, head_cols(h)] = (o.T * sg_ref[...] * (1.0 - lambda_init)).astype(BF16)


def _diff_attention(qt, k, vt, lq1, lk1, lq2, lk2, sg, batch, seq, lambda_init):
    n = k.shape[0]
    steps = seq // TQ
    qt_spec = pl.BlockSpec((None, HALF, TQ), lambda b, i: (b * steps + i, 0, 0))
    k_spec = pl.BlockSpec((seq, HALF), lambda b, i: (b, 0))
    vt_spec = pl.BlockSpec((seq // TK, HALF, TK), lambda b, i: (b, 0, 0))
    o_spec = pl.BlockSpec((TQ, HALF), lambda b, i: (b * steps + i, 0))
    return pl.pallas_call(
        functools.partial(_attn_kernel, lambda_init=lambda_init),
        grid=(batch, steps),
        in_specs=[qt_spec, k_spec, vt_spec, _const_spec(lq1.shape), _const_spec(lk1.shape),
                  _const_spec(lq2.shape), _const_spec(lk2.shape), _const_spec(sg.shape)],
        out_specs=o_spec,
        out_shape=jax.ShapeDtypeStruct((n, HALF), BF16),
        scratch_shapes=[pltpu.VMEM((C_HEADS, HEAD_W, 2 * TQ), BF16),
                        pltpu.VMEM((2, C_HEADS, TK, 2 * TQ), F32),
                        pltpu.VMEM((2, C_HEADS, 1, 2 * TQ), F32),
                        pltpu.VMEM((C_HEADS, 1, 2 * TQ), F32),
                        pltpu.VMEM((C_HEADS, HEAD_W + ONES_ROWS, 2 * TQ), F32)],
        compiler_params=pltpu.CompilerParams(
            dimension_semantics=("arbitrary", "arbitrary"), vmem_limit_bytes=VMEM_LIMIT),
        name="diff_attention",
    )(qt, k, vt, lq1, lk1, lq2, lk2, sg)


def kernel(x, even_w_in, even_conv_a_w, even_conv_b_w, even_conv_b_bias, even_conv_ln_g, even_conv_ln_b, even_w_out, odd_w_in, odd_lambda_q1, odd_lambda_k1, odd_lambda_q2, odd_lambda_k2, odd_subln_g, odd_gmlp_ln_g, odd_gmlp_ln_b, odd_spatial_w, odd_spatial_b, odd_w_out, mix_ln_g, mix_ln_b, ffn_w_gate, ffn_w_up, ffn_w_down, ffn_ln_g, ffn_ln_b):
    batch, seq, d = x.shape
    assert d == D_MODEL and seq % TILE == 0 and TILE % TQ == 0 and TQ == 2 * TK
    xf = x.reshape(batch * seq, d)
    row = lambda a: a.reshape(1, -1)
    bf = lambda a: a.astype(BF16)

    def post(layer, ya, yb, xres, w_out):
        return _post_block(ya, yb, xres, bf(w_out), row(mix_ln_g[layer]), row(mix_ln_b[layer]),
                           bf(ffn_w_gate[layer]), bf(ffn_w_up[layer]), bf(ffn_w_down[layer]),
                           row(ffn_ln_g[layer]), row(ffn_ln_b[layer]))

    ya, yb = _even_mixer(xf, bf(even_w_in[0]), even_conv_a_w[0], even_conv_b_w[0],
                         row(even_conv_b_bias[0]), row(even_conv_ln_g[0]), row(even_conv_ln_b[0]),
                         batch, seq)
    xf = post(0, ya, yb, xf, even_w_out[0])

    lambda_init = 0.8 - 0.6 * math.exp(-0.3 * 1)
    sb = jnp.broadcast_to(odd_spatial_b[0].T[:, :, None], (CHUNK, D_GROUPS, CHUNK)).reshape(CHUNK, HALF)
    w_in = bf(odd_w_in[0])
    wqv_t = jnp.concatenate([w_in[:, 0:HALF], w_in[:, 2 * HALF:3 * HALF]], axis=1).T
    qt, k, vt, yd = _odd_proj(xf, w_in, wqv_t, row(odd_gmlp_ln_g[0]), row(odd_gmlp_ln_b[0]),
                              odd_spatial_w[0], sb)
    yc = _diff_attention(qt, k, vt, row(odd_lambda_q1[0]), row(odd_lambda_k1[0]),
                         row(odd_lambda_q2[0]), row(odd_lambda_k2[0]), row(odd_subln_g[0]),
                         batch, seq, lambda_init)
    xf = post(1, yc, yd, xf, odd_w_out[0])
    return xf.reshape(batch, seq, d)
```

```python
import functools
import math

import jax
import jax.numpy as jnp
from jax import lax
from jax.experimental import pallas as pl
from jax.experimental.pallas import tpu as pltpu

F32 = jnp.float32
BF16 = jnp.bfloat16

D_MODEL = 1024
HALF = D_MODEL // 2
A_CONV = 3
B_CONV = 31
C_HEAD_DIM = 64
C_HEADS = 4
HEAD_W = 2 * C_HEAD_DIM
CHUNK = 128
D_GROUPS = 4
D_FF = 2816
DEPTH = 2
ALPHA = (2 * DEPTH) ** 0.25
LN_EPS = 1e-5
LOG2E = 1.4426950408889634

SUBLANES = 8
HALO_A = 8
HALO_B = 32
CONV_ROWS = 32

TILE = 512
FF_CHUNK = 256
TQ = 512
TK = 256
VMEM_LIMIT = 56 * 1024 * 1024


def _layer_norm(x, g, b):
    mu = jnp.mean(x, axis=-1, keepdims=True)
    xc = x - mu
    var = jnp.mean(xc * xc, axis=-1, keepdims=True)
    return xc * lax.rsqrt(var + LN_EPS) * g + b


def _sigmoid(x):
    return 1.0 / (1.0 + jnp.exp(-x))


def _dot(a, b):
    return jnp.dot(a, b, preferred_element_type=F32)


def _const_spec(shape):
    zeros = (0,) * len(shape)
    return pl.BlockSpec(shape, lambda *_: zeros, pipeline_mode=pl.Buffered(1))


def _causal_conv(ext_ref, shifted_ref, wrep_ref, first_row, rows, out_fn):
    def window(offset):
        r = offset % SUBLANES
        if shifted_ref is None or r == 0:
            return ext_ref[pl.ds(offset, CONV_ROWS), :]
        return shifted_ref[r - 1, pl.ds(offset - r, CONV_ROWS), :]

    def tap(k):
        return jnp.concatenate([wrep_ref[k]] * (CONV_ROWS // SUBLANES), axis=0)

    for r0 in range(0, rows, CONV_ROWS):
        acc = tap(0) * window(first_row + r0)
        for k in range(1, wrep_ref.shape[0]):
            acc = acc + tap(k) * window(first_row + r0 + k)
        out_fn(r0, acc)


def _replicate_taps(w_ref, wrep_ref):
    for k in range(w_ref.shape[0]):
        wrep_ref[k] = jnp.broadcast_to(w_ref[k:k + 1, :], (SUBLANES, w_ref.shape[1]))


def _even_mixer_kernel(x_ref, w_in_ref, wa_ref, wb_ref, bias_ref, g_ref, b_ref,
                       ya_ref, yb_ref, aext, zext, zshift, warep, wbrep, gate_scr, conv_scr):
    rows = x_ref.shape[0]

    @pl.when(pl.program_id(1) == 0)
    def _():
        aext[0:HALO_A, :] = jnp.zeros((HALO_A, HALF), F32)
        zext[0:HALO_B, :] = jnp.zeros((HALO_B, HALF), F32)
        _replicate_taps(wa_ref, warep)
        _replicate_taps(wb_ref, wbrep)

    xb = x_ref[...].astype(BF16)

    def proj(c):
        return _dot(xb, w_in_ref[:, c * HALF:(c + 1) * HALF])

    zext[HALO_B:HALO_B + rows, :] = proj(3) * _sigmoid(proj(4))
    gate_scr[...] = proj(0)
    aext[HALO_A:HALO_A + rows, :] = proj(1) * proj(2)

    for r in range(1, SUBLANES):
        zshift[r - 1] = zext[pl.ds(r, rows + HALO_B - SUBLANES), :]

    def store_b(r0, acc):
        conv_scr[pl.ds(r0, CONV_ROWS), :] = acc

    _causal_conv(zext, zshift, wbrep,
                 HALO_B - (B_CONV - 1), rows, store_b)
    zext[0:HALO_B, :] = zext[rows:rows + HALO_B, :]

    zn = _layer_norm(conv_scr[...] + bias_ref[...], g_ref[...], b_ref[...])
    yb_ref[...] = (zn * _sigmoid(zn)).astype(BF16)

    def store_a(r0, acc):
        ya_ref[pl.ds(r0, CONV_ROWS), :] = (gate_scr[pl.ds(r0, CONV_ROWS), :] * acc).astype(BF16)

    _causal_conv(aext, None, warep,
                 HALO_A - (A_CONV - 1), rows, store_a)
    aext[0:HALO_A, :] = aext[rows:rows + HALO_A, :]


def _even_mixer(x, w_in, wa, wb, bias, g, b, batch, seq):
    n = x.shape[0]
    steps = seq // TILE
    row_spec = lambda w: pl.BlockSpec((TILE, w), lambda bi, si: (bi * steps + si, 0))
    return pl.pallas_call(
        _even_mixer_kernel,
        grid=(batch, steps),
        in_specs=[row_spec(D_MODEL), _const_spec(w_in.shape), _const_spec(wa.shape),
                  _const_spec(wb.shape), _const_spec(bias.shape), _const_spec(g.shape),
                  _const_spec(b.shape)],
        out_specs=[row_spec(HALF), row_spec(HALF)],
        out_shape=[jax.ShapeDtypeStruct((n, HALF), BF16)] * 2,
        scratch_shapes=[pltpu.VMEM((TILE + HALO_A, HALF), F32),
                        pltpu.VMEM((TILE + HALO_B, HALF), F32),
                        pltpu.VMEM((SUBLANES - 1, TILE + HALO_B - SUBLANES, HALF), F32),
                        pltpu.VMEM((A_CONV, SUBLANES, HALF), F32),
                        pltpu.VMEM((B_CONV, SUBLANES, HALF), F32),
                        pltpu.VMEM((TILE, HALF), F32),
                        pltpu.VMEM((TILE, HALF), F32)],
        compiler_params=pltpu.CompilerParams(
            dimension_semantics=("arbitrary", "arbitrary"), vmem_limit_bytes=VMEM_LIMIT),
        name="even_mixer",
    )(x, w_in, wa, wb, bias, g, b)


def _post_kernel(ya_ref, yb_ref, x_ref, wout_ref, mg_ref, mb_ref, wg_ref, wu_ref, wd_ref,
                 fg_ref, fb_ref, o_ref, x1_scr, h_scr):
    rows = x_ref.shape[0]
    halves = (slice(0, rows // 2), slice(rows // 2, rows))

    mixed = [_dot(ya_ref[rws, :], wout_ref[0:HALF, :]) + _dot(yb_ref[rws, :], wout_ref[HALF:D_MODEL, :])
             for rws in halves]
    for rws, m in zip(halves, mixed):
        x1_scr[rws, :] = _layer_norm(ALPHA * x_ref[rws, :] + m, mg_ref[...], mb_ref[...])

    xb = x1_scr[...].astype(BF16)
    for c in range(0, D_FF, FF_CHUNK):
        gate = _dot(xb, wg_ref[:, c:c + FF_CHUNK])
        up = _dot(xb, wu_ref[:, c:c + FF_CHUNK])
        h_scr[:, c:c + FF_CHUNK] = (gate * _sigmoid(gate) * up).astype(BF16)

    ffn = [_dot(h_scr[rws, :], wd_ref[...]) for rws in halves]
    for rws, f in zip(halves, ffn):
        o_ref[rws, :] = _layer_norm(ALPHA * x1_scr[rws, :] + f, fg_ref[...], fb_ref[...])


def _post_block(ya, yb, x, wout, mg, mb, wg, wu, wd, fg, fb):
    n = x.shape[0]
    row_spec = lambda w: pl.BlockSpec((TILE, w), lambda i: (i, 0))
    return pl.pallas_call(
        _post_kernel,
        grid=(n // TILE,),
        in_specs=[row_spec(HALF), row_spec(HALF), row_spec(D_MODEL), _const_spec(wout.shape),
                  _const_spec(mg.shape), _const_spec(mb.shape), _const_spec(wg.shape),
                  _const_spec(wu.shape), _const_spec(wd.shape), _const_spec(fg.shape),
                  _const_spec(fb.shape)],
        out_specs=row_spec(D_MODEL),
        out_shape=jax.ShapeDtypeStruct((n, D_MODEL), F32),
        scratch_shapes=[pltpu.VMEM((TILE, D_MODEL), F32), pltpu.VMEM((TILE, D_FF), BF16)],
        compiler_params=pltpu.CompilerParams(
            dimension_semantics=("arbitrary",), vmem_limit_bytes=VMEM_LIMIT),
        name="post_block",
    )(ya, yb, x, wout, mg, mb, wg, wu, wd, fg, fb)


def _gelu_tanh(x):
    return 0.5 * x * (1.0 + jnp.tanh(math.sqrt(2.0 / math.pi) * (x + 0.044715 * (x * x * x))))


def _dot_nt(a, b):
    return lax.dot_general(a, b, (((1,), (1,)), ((), ())), preferred_element_type=F32)


def _odd_proj_kernel(x_ref, w_in_ref, wqv_t_ref, lg_ref, lb_ref, sw_ref, sb_ref,
                     qt_ref, k_ref, vt_ref, yd_ref):
    rows = x_ref.shape[0]
    xb = x_ref[...].astype(BF16)

    def proj(c):
        return _dot(xb, w_in_ref[:, c * HALF:(c + 1) * HALF])

    u = _gelu_tanh(proj(3))
    vg = _gelu_tanh(proj(4))

    q_t = (_dot_nt(wqv_t_ref[0:HALF, :], xb) * (C_HEAD_DIM ** -0.5 * LOG2E)).astype(BF16)
    v_t = _dot_nt(wqv_t_ref[HALF:2 * HALF, :], xb).astype(BF16)
    for t in range(rows // TQ):
        qt_ref[t] = q_t[:, t * TQ:(t + 1) * TQ]
    for t in range(rows // TK):
        vt_ref[t] = v_t[:, t * TK:(t + 1) * TK]
    k_ref[...] = proj(1).astype(BF16)

    row = lax.broadcasted_iota(jnp.int32, (CHUNK, CHUNK), 0)
    col = lax.broadcasted_iota(jnp.int32, (CHUNK, CHUNK), 1)
    for g in range(D_GROUPS):
        lanes = slice(g * CHUNK, (g + 1) * CHUNK)
        vn = _layer_norm(vg[:, lanes], lg_ref[:, lanes], lb_ref[:, lanes]).astype(BF16)
        w_causal = jnp.where(col <= row, sw_ref[g], 0.0).astype(BF16)
        for r0 in range(0, rows, CHUNK):
            sp = _dot(w_causal, vn[r0:r0 + CHUNK, :]) + sb_ref[:, lanes]
            yd_ref[r0:r0 + CHUNK, lanes] = (u[r0:r0 + CHUNK, lanes] * sp).astype(BF16)


def _odd_proj(x, w_in, wqv_t, lg, lb, sw, sb):
    n = x.shape[0]
    row_spec = lambda w: pl.BlockSpec((TILE, w), lambda i: (i, 0))
    t_spec = lambda blk: pl.BlockSpec((TILE // blk, HALF, blk), lambda i: (i, 0, 0))
    t_shape = lambda blk: jax.ShapeDtypeStruct((n // blk, HALF, blk), BF16)
    r_shape = jax.ShapeDtypeStruct((n, HALF), BF16)
    return pl.pallas_call(
        _odd_proj_kernel,
        grid=(n // TILE,),
        in_specs=[row_spec(D_MODEL), _const_spec(w_in.shape), _const_spec(wqv_t.shape),
                  _const_spec(lg.shape), _const_spec(lb.shape), _const_spec(sw.shape),
                  _const_spec(sb.shape)],
        out_specs=[t_spec(TQ), row_spec(HALF), t_spec(TK), row_spec(HALF)],
        out_shape=[t_shape(TQ), r_shape, t_shape(TK), r_shape],
        compiler_params=pltpu.CompilerParams(
            dimension_semantics=("arbitrary",), vmem_limit_bytes=VMEM_LIMIT),
        name="odd_proj",
    )(x, w_in, wqv_t, lg, lb, sw, sb)


ONES_ROWS = 16


def _attn_kernel(qt_ref, k_ref, vt_ref, lq1_ref, lk1_ref, lq2_ref, lk2_ref, sg_ref, o_ref,
                 qs_scr, s_scr, cmax_scr, m_scr, acc_scr, *, lambda_init):
    qi = pl.program_id(1)
    heads = range(C_HEADS)
    head_cols = lambda h: slice(h * HEAD_W, (h + 1) * HEAD_W)

    chan = lax.broadcasted_iota(jnp.int32, (HEAD_W, TQ), 0)
    zero = jnp.zeros((HEAD_W, TQ), BF16)
    for h in heads:
        q_t = qt_ref[head_cols(h), :]
        qs_scr[h, :, 0:TQ] = jnp.where(chan < C_HEAD_DIM, q_t, zero)
        qs_scr[h, :, TQ:2 * TQ] = jnp.where(chan >= C_HEAD_DIM, q_t, zero)
    m_scr[...] = jnp.full(m_scr.shape, -jnp.inf, F32)
    acc_scr[...] = jnp.zeros(acc_scr.shape, F32)
    ones = jnp.ones((ONES_ROWS, TK), BF16)

    def scores(t, slot, h):
        rows = pl.ds(pl.multiple_of(t * TK, TK), TK)
        s = _dot(k_ref[rows, head_cols(h)], qs_scr[h])
        s_scr[slot, h] = s
        cmax_scr[slot, h] = jnp.max(s, axis=0, keepdims=True)

    def fold(t, slot, h, keep):
        s = s_scr[slot, h]
        if keep is None:
            cmax = cmax_scr[slot, h]
        else:
            s = jnp.where(keep, s, -jnp.inf)
            cmax = jnp.max(s, axis=0, keepdims=True)
        m_old = m_scr[h]
        m_new = jnp.maximum(m_old, cmax)
        p = jnp.exp2(s - m_new).astype(BF16)
        v_ext = jnp.concatenate([vt_ref[t, head_cols(h), :], ones], axis=0)
        acc_scr[h] = jnp.exp2(m_old - m_new) * acc_scr[h] + _dot(v_ext, p)
        m_scr[h] = m_new

    def step(t, slot, keep=None, prefetch=True):
        for h in range(C_HEADS + 1):
            if prefetch and h < C_HEADS:
                scores(t + 1, 1 - slot, h)
            if h > 0:
                fold(t, slot, h - 1, keep)

    for h in heads:
        scores(0, 0, h)

    def pair(u, carry):
        step(2 * u, 0)
        step(2 * u + 1, 1)
        return carry

    lax.fori_loop(0, qi, pair, 0)

    key = lax.broadcasted_iota(jnp.int32, (TK, TQ), 0)
    qry = lax.broadcasted_iota(jnp.int32, (TK, TQ), 1)
    step(2 * qi, 0, keep=jnp.concatenate([key <= qry] * 2, axis=1))
    step(2 * qi + 1, 1, keep=jnp.concatenate([key + TK <= qry] * 2, axis=1), prefetch=False)

    lam = (jnp.exp(jnp.sum(lq1_ref[...] * lk1_ref[...], axis=-1, keepdims=True))
           - jnp.exp(jnp.sum(lq2_ref[...] * lk2_ref[...], axis=-1, keepdims=True))
           + lambda_init)
    for h in heads:
        o = acc_scr[h, 0:HEAD_W, :] / acc_scr[h, HEAD_W:HEAD_W + 1, :]
        o = o[:, 0:TQ] - lam * o[:, TQ:2 * TQ]
        o = o * lax.rsqrt(jnp.mean(o * o, axis=0, keepdims=True) + LN_EPS)
        o_ref[:, head_cols(h)] = (o.T * sg_ref[...] * (1.0 - lambda_init)).astype(BF16)


def _diff_attention(qt, k, vt, lq1, lk1, lq2, lk2, sg, batch, seq, lambda_init):
    n = k.shape[0]
    steps = seq // TQ
    qt_spec = pl.BlockSpec((None, HALF, TQ), lambda b, i: (b * steps + i, 0, 0))
    k_spec = pl.BlockSpec((seq, HALF), lambda b, i: (b, 0))
    vt_spec = pl.BlockSpec((seq // TK, HALF, TK), lambda b, i: (b, 0, 0))
    o_spec = pl.BlockSpec((TQ, HALF), lambda b, i: (b * steps + i, 0))
    return pl.pallas_call(
        functools.partial(_attn_kernel, lambda_init=lambda_init),
        grid=(batch, steps),
        in_specs=[qt_spec, k_spec, vt_spec, _const_spec(lq1.shape), _const_spec(lk1.shape),
                  _const_spec(lq2.shape), _const_spec(lk2.shape), _const_spec(sg.shape)],
        out_specs=o_spec,
        out_shape=jax.ShapeDtypeStruct((n, HALF), BF16),
        scratch_shapes=[pltpu.VMEM((C_HEADS, HEAD_W, 2 * TQ), BF16),
                        pltpu.VMEM((2, C_HEADS, TK, 2 * TQ), F32),
                        pltpu.VMEM((2, C_HEADS, 1, 2 * TQ), F32),
                        pltpu.VMEM((C_HEADS, 1, 2 * TQ), F32),
                        pltpu.VMEM((C_HEADS, HEAD_W + ONES_ROWS, 2 * TQ), F32)],
        compiler_params=pltpu.CompilerParams(
            dimension_semantics=("arbitrary", "arbitrary"), vmem_limit_bytes=VMEM_LIMIT),
        name="diff_attention",
    )(qt, k, vt, lq1, lk1, lq2, lk2, sg)


def kernel(x, even_w_in, even_conv_a_w, even_conv_b_w, even_conv_b_bias, even_conv_ln_g, even_conv_ln_b, even_w_out, odd_w_in, odd_lambda_q1, odd_lambda_k1, odd_lambda_q2, odd_lambda_k2, odd_subln_g, odd_gmlp_ln_g, odd_gmlp_ln_b, odd_spatial_w, odd_spatial_b, odd_w_out, mix_ln_g, mix_ln_b, ffn_w_gate, ffn_w_up, ffn_w_down, ffn_ln_g, ffn_ln_b):
    batch, seq, d = x.shape
    assert d == D_MODEL and seq % TILE == 0 and TILE % TQ == 0 and TQ == 2 * TK
    xf = x.reshape(batch * seq, d)
    row = lambda a: a.reshape(1, -1)
    bf = lambda a: a.astype(BF16)

    def post(layer, ya, yb, xres, w_out):
        return _post_block(ya, yb, xres, bf(w_out), row(mix_ln_g[layer]), row(mix_ln_b[layer]),
                           bf(ffn_w_gate[layer]), bf(ffn_w_up[layer]), bf(ffn_w_down[layer]),
                           row(ffn_ln_g[layer]), row(ffn_ln_b[layer]))

    ya, yb = _even_mixer(xf, bf(even_w_in[0]), even_conv_a_w[0], even_conv_b_w[0],
                         row(even_conv_b_bias[0]), row(even_conv_ln_g[0]), row(even_conv_ln_b[0]),
                         batch, seq)
    xf = post(0, ya, yb, xf, even_w_out[0])

    lambda_init = 0.8 - 0.6 * math.exp(-0.3 * 1)
    sb = jnp.broadcast_to(odd_spatial_b[0].T[:, :, None], (CHUNK, D_GROUPS, CHUNK)).reshape(CHUNK, HALF)
    w_in = bf(odd_w_in[0])
    wqv_t = jnp.concatenate([w_in[:, 0:HALF], w_in[:, 2 * HALF:3 * HALF]], axis=1).T
    qt, k, vt, yd = _odd_proj(xf, w_in, wqv_t, row(odd_gmlp_ln_g[0]), row(odd_gmlp_ln_b[0]),
                              odd_spatial_w[0], sb)
    yc = _diff_attention(qt, k, vt, row(odd_lambda_q1[0]), row(odd_lambda_k1[0]),
                         row(odd_lambda_q2[0]), row(odd_lambda_k2[0]), row(odd_subln_g[0]),
                         batch, seq, lambda_init)
    xf = post(1, yc, yd, xf, odd_w_out[0])
    return xf.reshape(batch, seq, d)
```

```python
import functools
import math

import jax
import jax.numpy as jnp
from jax import lax
from jax.experimental import pallas as pl
from jax.experimental.pallas import tpu as pltpu

F32 = jnp.float32
BF16 = jnp.bfloat16

D_MODEL = 1024
HALF = D_MODEL // 2
A_CONV = 3
B_CONV = 31
C_HEAD_DIM = 64
C_HEADS = 4
HEAD_W = 2 * C_HEAD_DIM
CHUNK = 128
D_GROUPS = 4
D_FF = 2816
DEPTH = 2
ALPHA = (2 * DEPTH) ** 0.25
LN_EPS = 1e-5
LOG2E = 1.4426950408889634

SUBLANES = 8
HALO_A = 8
HALO_B = 32
CONV_ROWS = 32

TILE = 512
FF_CHUNK = 256
POST_PARTS = 2
TQ = 512
TK = 256
VMEM_LIMIT = 56 * 1024 * 1024


def _layer_norm(x, g, b):
    mu = jnp.mean(x, axis=-1, keepdims=True)
    xc = x - mu
    var = jnp.mean(xc * xc, axis=-1, keepdims=True)
    return xc * lax.rsqrt(var + LN_EPS) * g + b


def _sigmoid(x):
    return 1.0 / (1.0 + jnp.exp(-x))


def _dot(a, b):
    return jnp.dot(a, b, preferred_element_type=F32)


def _const_spec(shape):
    zeros = (0,) * len(shape)
    return pl.BlockSpec(shape, lambda *_: zeros, pipeline_mode=pl.Buffered(1))


def _causal_conv(ext_ref, shifted_ref, w_rows, first_row, rows, out_fn):
    def window(offset):
        r = offset % SUBLANES
        if shifted_ref is None or r == 0:
            return ext_ref[pl.ds(offset, CONV_ROWS), :]
        return shifted_ref[r - 1, pl.ds(offset - r, CONV_ROWS), :]

    for r0 in range(0, rows, CONV_ROWS):
        acc = w_rows[0] * window(first_row + r0)
        for k in range(1, len(w_rows)):
            acc = acc + w_rows[k] * window(first_row + r0 + k)
        out_fn(r0, acc)


def _replicated_taps(w_ref, wrep_ref, taps):
    for k in range(taps):
        wrep_ref[k] = jnp.broadcast_to(w_ref[k:k + 1, :], (SUBLANES, w_ref.shape[1]))
    return [jnp.concatenate([wrep_ref[k]] * (CONV_ROWS // SUBLANES), axis=0) for k in range(taps)]


def _even_mixer_kernel(x_ref, w_in_ref, wa_ref, wb_ref, bias_ref, g_ref, b_ref,
                       ya_ref, yb_ref, aext, zext, zshift, warep, wbrep, gate_scr, conv_scr):
    rows = x_ref.shape[0]

    @pl.when(pl.program_id(1) == 0)
    def _():
        aext[0:HALO_A, :] = jnp.zeros((HALO_A, HALF), F32)
        zext[0:HALO_B, :] = jnp.zeros((HALO_B, HALF), F32)

    xb = x_ref[...].astype(BF16)

    def proj(c):
        return _dot(xb, w_in_ref[:, c * HALF:(c + 1) * HALF])

    zext[HALO_B:HALO_B + rows, :] = proj(3) * _sigmoid(proj(4))
    gate_scr[...] = proj(0)
    aext[HALO_A:HALO_A + rows, :] = proj(1) * proj(2)

    for r in range(1, SUBLANES):
        zshift[r - 1] = zext[pl.ds(r, rows + HALO_B - SUBLANES), :]

    def store_b(r0, acc):
        conv_scr[pl.ds(r0, CONV_ROWS), :] = acc

    _causal_conv(zext, zshift, _replicated_taps(wb_ref, wbrep, B_CONV),
                 HALO_B - (B_CONV - 1), rows, store_b)
    zext[0:HALO_B, :] = zext[rows:rows + HALO_B, :]

    zn = _layer_norm(conv_scr[...] + bias_ref[...], g_ref[...], b_ref[...])
    yb_ref[...] = (zn * _sigmoid(zn)).astype(BF16)

    def store_a(r0, acc):
        ya_ref[pl.ds(r0, CONV_ROWS), :] = (gate_scr[pl.ds(r0, CONV_ROWS), :] * acc).astype(BF16)

    _causal_conv(aext, None, _replicated_taps(wa_ref, warep, A_CONV),
                 HALO_A - (A_CONV - 1), rows, store_a)
    aext[0:HALO_A, :] = aext[rows:rows + HALO_A, :]


def _even_mixer(x, w_in, wa, wb, bias, g, b, batch, seq):
    n = x.shape[0]
    steps = seq // TILE
    row_spec = lambda w: pl.BlockSpec((TILE, w), lambda bi, si: (bi * steps + si, 0))
    return pl.pallas_call(
        _even_mixer_kernel,
        grid=(batch, steps),
        in_specs=[row_spec(D_MODEL), _const_spec(w_in.shape), _const_spec(wa.shape),
                  _const_spec(wb.shape), _const_spec(bias.shape), _const_spec(g.shape),
                  _const_spec(b.shape)],
        out_specs=[row_spec(HALF), row_spec(HALF)],
        out_shape=[jax.ShapeDtypeStruct((n, HALF), BF16)] * 2,
        scratch_shapes=[pltpu.VMEM((TILE + HALO_A, HALF), F32),
                        pltpu.VMEM((TILE + HALO_B, HALF), F32),
                        pltpu.VMEM((SUBLANES - 1, TILE + HALO_B - SUBLANES, HALF), F32),
                        pltpu.VMEM((A_CONV, SUBLANES, HALF), F32),
                        pltpu.VMEM((B_CONV, SUBLANES, HALF), F32),
                        pltpu.VMEM((TILE, HALF), F32),
                        pltpu.VMEM((TILE, HALF), F32)],
        compiler_params=pltpu.CompilerParams(
            dimension_semantics=("arbitrary", "arbitrary"), vmem_limit_bytes=VMEM_LIMIT),
        name="even_mixer",
    )(x, w_in, wa, wb, bias, g, b)


def _post_kernel(ya_ref, yb_ref, x_ref, wout_ref, mg_ref, mb_ref, wg_ref, wu_ref, wd_ref,
                 fg_ref, fb_ref, o_ref, x1_scr, h_scr):
    rows = x_ref.shape[0]
    parts = [slice(r, r + rows // POST_PARTS) for r in range(0, rows, rows // POST_PARTS)]
    carried = {}

    def out_proj(p):
        rws = parts[p]
        carried[p] = (_dot(ya_ref[rws, :], wout_ref[0:HALF, :])
                      + _dot(yb_ref[rws, :], wout_ref[HALF:D_MODEL, :]))

    def mix_norm(p):
        rws = parts[p]
        x1_scr[rws, :] = _layer_norm(ALPHA * x_ref[rws, :] + carried.pop(p), mg_ref[...], mb_ref[...])

    def gate_up(p):
        rws = parts[p]
        xb = x1_scr[rws, :].astype(BF16)
        for c in range(0, D_FF, FF_CHUNK):
            gate = _dot(xb, wg_ref[:, c:c + FF_CHUNK])
            up = _dot(xb, wu_ref[:, c:c + FF_CHUNK])
            h_scr[rws, c:c + FF_CHUNK] = (gate * _sigmoid(gate) * up).astype(BF16)

    def down_proj(p):
        carried[p] = _dot(h_scr[parts[p], :], wd_ref[...])

    def ffn_norm(p):
        rws = parts[p]
        o_ref[rws, :] = _layer_norm(ALPHA * x1_scr[rws, :] + carried.pop(p), fg_ref[...], fb_ref[...])

    stages = (out_proj, mix_norm, gate_up, down_proj, ffn_norm)
    for tick in range(len(stages) + POST_PARTS - 1):
        for p in range(POST_PARTS):
            if 0 <= tick - p < len(stages):
                stages[tick - p](p)


def _post_block(ya, yb, x, wout, mg, mb, wg, wu, wd, fg, fb):
    n = x.shape[0]
    row_spec = lambda w: pl.BlockSpec((TILE, w), lambda i: (i, 0))
    return pl.pallas_call(
        _post_kernel,
        grid=(n // TILE,),
        in_specs=[row_spec(HALF), row_spec(HALF), row_spec(D_MODEL), _const_spec(wout.shape),
                  _const_spec(mg.shape), _const_spec(mb.shape), _const_spec(wg.shape),
                  _const_spec(wu.shape), _const_spec(wd.shape), _const_spec(fg.shape),
                  _const_spec(fb.shape)],
        out_specs=row_spec(D_MODEL),
        out_shape=jax.ShapeDtypeStruct((n, D_MODEL), F32),
        scratch_shapes=[pltpu.VMEM((TILE, D_MODEL), F32), pltpu.VMEM((TILE, D_FF), BF16)],
        compiler_params=pltpu.CompilerParams(
            dimension_semantics=("arbitrary",), vmem_limit_bytes=VMEM_LIMIT),
        name="post_block",
    )(ya, yb, x, wout, mg, mb, wg, wu, wd, fg, fb)


def _gelu_tanh(x):
    return 0.5 * x * (1.0 + jnp.tanh(math.sqrt(2.0 / math.pi) * (x + 0.044715 * (x * x * x))))


def _dot_nt(a, b):
    return lax.dot_general(a, b, (((1,), (1,)), ((), ())), preferred_element_type=F32)


def _odd_proj_kernel(x_ref, w_in_ref, wqv_t_ref, lg_ref, lb_ref, sw_ref, sb_ref,
                     qt_ref, k_ref, vt_ref, yd_ref):
    rows = x_ref.shape[0]
    xb = x_ref[...].astype(BF16)

    def proj(c):
        return _dot(xb, w_in_ref[:, c * HALF:(c + 1) * HALF])

    u = _gelu_tanh(proj(3))
    vg = _gelu_tanh(proj(4))

    q_t = (_dot_nt(wqv_t_ref[0:HALF, :], xb) * (C_HEAD_DIM ** -0.5 * LOG2E)).astype(BF16)
    v_t = _dot_nt(wqv_t_ref[HALF:2 * HALF, :], xb).astype(BF16)
    for t in range(rows // TQ):
        qt_ref[t] = q_t[:, t * TQ:(t + 1) * TQ]
    for t in range(rows // TK):
        vt_ref[t] = v_t[:, t * TK:(t + 1) * TK]
    k_ref[...] = proj(1).astype(BF16)

    row = lax.broadcasted_iota(jnp.int32, (CHUNK, CHUNK), 0)
    col = lax.broadcasted_iota(jnp.int32, (CHUNK, CHUNK), 1)
    for g in range(D_GROUPS):
        lanes = slice(g * CHUNK, (g + 1) * CHUNK)
        vn = _layer_norm(vg[:, lanes], lg_ref[:, lanes], lb_ref[:, lanes]).astype(BF16)
        w_causal = jnp.where(col <= row, sw_ref[g], 0.0).astype(BF16)
        for r0 in range(0, rows, CHUNK):
            sp = _dot(w_causal, vn[r0:r0 + CHUNK, :]) + sb_ref[:, lanes]
            yd_ref[r0:r0 + CHUNK, lanes] = (u[r0:r0 + CHUNK, lanes] * sp).astype(BF16)


def _odd_proj(x, w_in, wqv_t, lg, lb, sw, sb):
    n = x.shape[0]
    row_spec = lambda w: pl.BlockSpec((TILE, w), lambda i: (i, 0))
    t_spec = lambda blk: pl.BlockSpec((TILE // blk, HALF, blk), lambda i: (i, 0, 0))
    t_shape = lambda blk: jax.ShapeDtypeStruct((n // blk, HALF, blk), BF16)
    r_shape = jax.ShapeDtypeStruct((n, HALF), BF16)
    return pl.pallas_call(
        _odd_proj_kernel,
        grid=(n // TILE,),
        in_specs=[row_spec(D_MODEL), _const_spec(w_in.shape), _const_spec(wqv_t.shape),
                  _const_spec(lg.shape), _const_spec(lb.shape), _const_spec(sw.shape),
                  _const_spec(sb.shape)],
        out_specs=[t_spec(TQ), row_spec(HALF), t_spec(TK), row_spec(HALF)],
        out_shape=[t_shape(TQ), r_shape, t_shape(TK), r_shape],
        compiler_params=pltpu.CompilerParams(
            dimension_semantics=("arbitrary",), vmem_limit_bytes=VMEM_LIMIT),
        name="odd_proj",
    )(x, w_in, wqv_t, lg, lb, sw, sb)


ONES_ROWS = 16


def _attn_kernel(qt_ref, k_ref, vt_ref, lq1_ref, lk1_ref, lq2_ref, lk2_ref, sg_ref, o_ref,
                 qs_scr, s_scr, cmax_scr, m_scr, acc_scr, *, lambda_init):
    qi = pl.program_id(1)
    heads = range(C_HEADS)
    head_cols = lambda h: slice(h * HEAD_W, (h + 1) * HEAD_W)

    chan = lax.broadcasted_iota(jnp.int32, (HEAD_W, TQ), 0)
    zero = jnp.zeros((HEAD_W, TQ), BF16)
    for h in heads:
        q_t = qt_ref[head_cols(h), :]
        qs_scr[h, :, 0:TQ] = jnp.where(chan < C_HEAD_DIM, q_t, zero)
        qs_scr[h, :, TQ:2 * TQ] = jnp.where(chan >= C_HEAD_DIM, q_t, zero)
    m_scr[...] = jnp.full(m_scr.shape, -jnp.inf, F32)
    acc_scr[...] = jnp.zeros(acc_scr.shape, F32)
    ones = jnp.ones((ONES_ROWS, TK), BF16)

    all_cols = (slice(0, 2 * TQ),)
    late_cols = (slice(TK, TQ), slice(TQ + TK, 2 * TQ))

    def gather(ref_row, cols):
        return ref_row if cols is all_cols else jnp.concatenate([ref_row[:, c] for c in cols], axis=1)

    def scores(t, slot, h, cols):
        rows = pl.ds(pl.multiple_of(t * TK, TK), TK)
        s = _dot(k_ref[rows, head_cols(h)], gather(qs_scr[h], cols))
        s_scr[slot, h, :, 0:s.shape[1]] = s
        if cols is all_cols:
            cmax_scr[slot, h] = jnp.max(s, axis=0, keepdims=True)

    def fold(t, slot, h, keep, cols):
        width = sum(c.stop - c.start for c in cols)
        s = s_scr[slot, h, :, 0:width]
        if keep is None:
            cmax = cmax_scr[slot, h]
        else:
            s = jnp.where(keep, s, -jnp.inf)
            cmax = jnp.max(s, axis=0, keepdims=True)
        m_old = gather(m_scr[h], cols)
        m_new = jnp.maximum(m_old, cmax)
        p = jnp.exp2(s - m_new).astype(BF16)
        v_ext = jnp.concatenate([vt_ref[t, head_cols(h), :], ones], axis=0)
        alpha = jnp.exp2(m_old - m_new)
        pv = _dot(v_ext, p)
        at = 0
        for c in cols:
            part = slice(at, at + c.stop - c.start)
            acc_scr[h, :, c] = alpha[:, part] * acc_scr[h, :, c] + pv[:, part]
            m_scr[h, :, c] = m_new[:, part]
            at = part.stop

    def step(t, slot, keep=None, cols=all_cols, next_cols=all_cols):
        for h in range(C_HEADS + 1):
            if next_cols is not None and h < C_HEADS:
                scores(t + 1, 1 - slot, h, next_cols)
            if h > 0:
                fold(t, slot, h - 1, keep, cols)

    for h in heads:
        scores(0, 0, h, all_cols)

    def pair(u, carry):
        step(2 * u, 0)
        step(2 * u + 1, 1)
        return carry

    lax.fori_loop(0, qi, pair, 0)

    key = lax.broadcasted_iota(jnp.int32, (TK, TQ), 0)
    qry = lax.broadcasted_iota(jnp.int32, (TK, TQ), 1)
    causal = key <= qry
    step(2 * qi, 0, keep=jnp.concatenate([causal] * 2, axis=1), next_cols=late_cols)
    step(2 * qi + 1, 1, keep=jnp.concatenate([causal[:, 0:TQ - TK]] * 2, axis=1),
         cols=late_cols, next_cols=None)

    lam = (jnp.exp(jnp.sum(lq1_ref[...] * lk1_ref[...], axis=-1, keepdims=True))
           - jnp.exp(jnp.sum(lq2_ref[...] * lk2_ref[...], axis=-1, keepdims=True))
           + lambda_init)
    for h in heads:
        o = acc_scr[h, 0:HEAD_W, :] / acc_scr[h, HEAD_W:HEAD_W + 1, :]
        o = o[:, 0:TQ] - lam * o[:, TQ:2 * TQ]
        o = o * lax.rsqrt(jnp.mean(o * o, axis=0, keepdims=True) + LN_EPS)
        o_ref[:, head_cols(h)] = (o.T * sg_ref[...] * (1.0 - lambda_init)).astype(BF16)


def _diff_attention(qt, k, vt, lq1, lk1, lq2, lk2, sg, batch, seq, lambda_init):
    n = k.shape[0]
    steps = seq // TQ
    qt_spec = pl.BlockSpec((None, HALF, TQ), lambda b, i: (b * steps + i, 0, 0))
    k_spec = pl.BlockSpec((seq, HALF), lambda b, i: (b, 0))
    vt_spec = pl.BlockSpec((seq // TK, HALF, TK), lambda b, i: (b, 0, 0))
    o_spec = pl.BlockSpec((TQ, HALF), lambda b, i: (b * steps + i, 0))
    return pl.pallas_call(
        functools.partial(_attn_kernel, lambda_init=lambda_init),
        grid=(batch, steps),
        in_specs=[qt_spec, k_spec, vt_spec, _const_spec(lq1.shape), _const_spec(lk1.shape),
                  _const_spec(lq2.shape), _const_spec(lk2.shape), _const_spec(sg.shape)],
        out_specs=o_spec,
        out_shape=jax.ShapeDtypeStruct((n, HALF), BF16),
        scratch_shapes=[pltpu.VMEM((C_HEADS, HEAD_W, 2 * TQ), BF16),
                        pltpu.VMEM((2, C_HEADS, TK, 2 * TQ), F32),
                        pltpu.VMEM((2, C_HEADS, 1, 2 * TQ), F32),
                        pltpu.VMEM((C_HEADS, 1, 2 * TQ), F32),
                        pltpu.VMEM((C_HEADS, HEAD_W + ONES_ROWS, 2 * TQ), F32)],
        compiler_params=pltpu.CompilerParams(
            dimension_semantics=("arbitrary", "arbitrary"), vmem_limit_bytes=VMEM_LIMIT),
        name="diff_attention",
    )(qt, k, vt, lq1, lk1, lq2, lk2, sg)


def kernel(x, even_w_in, even_conv_a_w, even_conv_b_w, even_conv_b_bias, even_conv_ln_g, even_conv_ln_b, even_w_out, odd_w_in, odd_lambda_q1, odd_lambda_k1, odd_lambda_q2, odd_lambda_k2, odd_subln_g, odd_gmlp_ln_g, odd_gmlp_ln_b, odd_spatial_w, odd_spatial_b, odd_w_out, mix_ln_g, mix_ln_b, ffn_w_gate, ffn_w_up, ffn_w_down, ffn_ln_g, ffn_ln_b):
    batch, seq, d = x.shape
    assert d == D_MODEL and seq % TILE == 0 and TILE % TQ == 0 and TQ == 2 * TK
    xf = x.reshape(batch * seq, d)
    row = lambda a: a.reshape(1, -1)
    bf = lambda a: a.astype(BF16)

    def post(layer, ya, yb, xres, w_out):
        return _post_block(ya, yb, xres, bf(w_out), row(mix_ln_g[layer]), row(mix_ln_b[layer]),
                           bf(ffn_w_gate[layer]), bf(ffn_w_up[layer]), bf(ffn_w_down[layer]),
                           row(ffn_ln_g[layer]), row(ffn_ln_b[layer]))

    ya, yb = _even_mixer(xf, bf(even_w_in[0]), even_conv_a_w[0], even_conv_b_w[0],
                         row(even_conv_b_bias[0]), row(even_conv_ln_g[0]), row(even_conv_ln_b[0]),
                         batch, seq)
    xf = post(0, ya, yb, xf, even_w_out[0])

    lambda_init = 0.8 - 0.6 * math.exp(-0.3 * 1)
    sb = jnp.broadcast_to(odd_spatial_b[0].T[:, :, None], (CHUNK, D_GROUPS, CHUNK)).reshape(CHUNK, HALF)
    w_in = bf(odd_w_in[0])
    wqv_t = jnp.concatenate([w_in[:, 0:HALF], w_in[:, 2 * HALF:3 * HALF]], axis=1).T
    qt, k, vt, yd = _odd_proj(xf, w_in, wqv_t, row(odd_gmlp_ln_g[0]), row(odd_gmlp_ln_b[0]),
                              odd_spatial_w[0], sb)
    yc = _diff_attention(qt, k, vt, row(odd_lambda_q1[0]), row(odd_lambda_k1[0]),
                         row(odd_lambda_q2[0]), row(odd_lambda_k2[0]), row(odd_subln_g[0]),
                         batch, seq, lambda_init)
    xf = post(1, yc, yd, xf, odd_w_out[0])
    return xf.reshape(batch, seq, d)
```

```python
import functools
import math

import jax
import jax.numpy as jnp
from jax import lax
from jax.experimental import pallas as pl
from jax.experimental.pallas import tpu as pltpu

F32 = jnp.float32
BF16 = jnp.bfloat16

D_MODEL = 1024
HALF = D_MODEL // 2
A_CONV = 3
B_CONV = 31
C_HEAD_DIM = 64
C_HEADS = 4
HEAD_W = 2 * C_HEAD_DIM
CHUNK = 128
D_GROUPS = 4
D_FF = 2816
DEPTH = 2
ALPHA = (2 * DEPTH) ** 0.25
LN_EPS = 1e-5
LOG2E = 1.4426950408889634

SUBLANES = 8
HALO_A = 8
HALO_B = 32
CONV_ROWS = 32

TILE = 512
FF_CHUNK = 256
CAST_SPLIT = 2
POST_TILE = 1024
POST_PARTS = 4
TQ = 512
TK = 256
VMEM_LIMIT = 56 * 1024 * 1024


def _layer_norm(x, g, b):
    mu = jnp.mean(x, axis=-1, keepdims=True)
    xc = x - mu
    var = jnp.mean(xc * xc, axis=-1, keepdims=True)
    return xc * lax.rsqrt(var + LN_EPS) * g + b


def _sigmoid(x):
    return 1.0 / (1.0 + jnp.exp(-x))


def _dot(a, b):
    return jnp.dot(a, b, preferred_element_type=F32)


def _const_spec(shape):
    zeros = (0,) * len(shape)
    return pl.BlockSpec(shape, lambda *_: zeros, pipeline_mode=pl.Buffered(1))


def _causal_conv(ext_ref, shifted_ref, w_rows, first_row, rows, out_fn):
    def window(offset):
        r = offset % SUBLANES
        if shifted_ref is None or r == 0:
            return ext_ref[pl.ds(offset, CONV_ROWS), :]
        return shifted_ref[r - 1, pl.ds(offset - r, CONV_ROWS), :]

    for r0 in range(0, rows, CONV_ROWS):
        acc = w_rows[0] * window(first_row + r0)
        for k in range(1, len(w_rows)):
            acc = acc + w_rows[k] * window(first_row + r0 + k)
        out_fn(r0, acc)


def _replicated_taps(w_ref, wrep_ref, taps):
    for k in range(taps):
        wrep_ref[k] = jnp.broadcast_to(w_ref[k:k + 1, :], (SUBLANES, w_ref.shape[1]))
    return [jnp.concatenate([wrep_ref[k]] * (CONV_ROWS // SUBLANES), axis=0) for k in range(taps)]


def _even_mixer_kernel(x_ref, w_in_ref, wa_ref, wb_ref, bias_ref, g_ref, b_ref,
                       ya_ref, yb_ref, aext, zext, zshift, warep, wbrep, gate_scr, conv_scr):
    rows = x_ref.shape[0]

    @pl.when(pl.program_id(1) == 0)
    def _():
        aext[0:HALO_A, :] = jnp.zeros((HALO_A, HALF), F32)
        zext[0:HALO_B, :] = jnp.zeros((HALO_B, HALF), F32)

    xb = x_ref[...].astype(BF16)

    def proj(c):
        return _dot(xb, w_in_ref[:, c * HALF:(c + 1) * HALF])

    zext[HALO_B:HALO_B + rows, :] = proj(3) * _sigmoid(proj(4))
    gate_scr[...] = proj(0)
    aext[HALO_A:HALO_A + rows, :] = proj(1) * proj(2)

    for r in range(1, SUBLANES):
        zshift[r - 1] = zext[pl.ds(r, rows + HALO_B - SUBLANES), :]

    def store_b(r0, acc):
        conv_scr[pl.ds(r0, CONV_ROWS), :] = acc

    _causal_conv(zext, zshift, _replicated_taps(wb_ref, wbrep, B_CONV),
                 HALO_B - (B_CONV - 1), rows, store_b)
    zext[0:HALO_B, :] = zext[rows:rows + HALO_B, :]

    zn = _layer_norm(conv_scr[...] + bias_ref[...], g_ref[...], b_ref[...])
    yb_ref[...] = (zn * _sigmoid(zn)).astype(BF16)

    def store_a(r0, acc):
        ya_ref[pl.ds(r0, CONV_ROWS), :] = (gate_scr[pl.ds(r0, CONV_ROWS), :] * acc).astype(BF16)

    _causal_conv(aext, None, _replicated_taps(wa_ref, warep, A_CONV),
                 HALO_A - (A_CONV - 1), rows, store_a)
    aext[0:HALO_A, :] = aext[rows:rows + HALO_A, :]


def _even_mixer(x, w_in, wa, wb, bias, g, b, batch, seq):
    n = x.shape[0]
    steps = seq // TILE
    row_spec = lambda w: pl.BlockSpec((TILE, w), lambda bi, si: (bi * steps + si, 0))
    return pl.pallas_call(
        _even_mixer_kernel,
        grid=(batch, steps),
        in_specs=[row_spec(D_MODEL), _const_spec(w_in.shape), _const_spec(wa.shape),
                  _const_spec(wb.shape), _const_spec(bias.shape), _const_spec(g.shape),
                  _const_spec(b.shape)],
        out_specs=[row_spec(HALF), row_spec(HALF)],
        out_shape=[jax.ShapeDtypeStruct((n, HALF), BF16)] * 2,
        scratch_shapes=[pltpu.VMEM((TILE + HALO_A, HALF), F32),
                        pltpu.VMEM((TILE + HALO_B, HALF), F32),
                        pltpu.VMEM((SUBLANES - 1, TILE + HALO_B - SUBLANES, HALF), F32),
                        pltpu.VMEM((A_CONV, SUBLANES, HALF), F32),
                        pltpu.VMEM((B_CONV, SUBLANES, HALF), F32),
                        pltpu.VMEM((TILE, HALF), F32),
                        pltpu.VMEM((TILE, HALF), F32)],
        compiler_params=pltpu.CompilerParams(
            dimension_semantics=("arbitrary", "arbitrary"), vmem_limit_bytes=VMEM_LIMIT),
        name="even_mixer",
    )(x, w_in, wa, wb, bias, g, b)


def _post_kernel(ya_ref, yb_ref, x_ref, wout_ref, mg_ref, mb_ref, wg_ref, wu_ref, wd_ref,
                 fg_ref, fb_ref, o_ref, x1_scr, h_scr):
    rows = x_ref.shape[0]
    parts = [slice(r, r + rows // POST_PARTS) for r in range(0, rows, rows // POST_PARTS)]
    carried = {}

    def out_proj(p):
        rws = parts[p]
        carried[p] = (_dot(ya_ref[rws, :], wout_ref[0:HALF, :])
                      + _dot(yb_ref[rws, :], wout_ref[HALF:D_MODEL, :]))

    def mix_norm(p):
        rws = parts[p]
        x1_scr[rws, :] = _layer_norm(ALPHA * x_ref[rws, :] + carried.pop(p), mg_ref[...], mb_ref[...])

    def gate_up(p):
        rws = parts[p]
        xb = x1_scr[rws, :].astype(BF16)
        for c in range(0, D_FF, FF_CHUNK):
            gate = _dot(xb, wg_ref[:, c:c + FF_CHUNK])
            up = _dot(xb, wu_ref[:, c:c + FF_CHUNK])
            h_scr[rws, c:c + FF_CHUNK] = (gate * _sigmoid(gate) * up).astype(BF16)

    def down_proj(p):
        carried[p] = _dot(h_scr[parts[p], :], wd_ref[...])

    def ffn_norm(p):
        rws = parts[p]
        o_ref[rws, :] = _layer_norm(ALPHA * x1_scr[rws, :] + carried.pop(p), fg_ref[...], fb_ref[...])

    stages = (out_proj, mix_norm, gate_up, down_proj, ffn_norm)
    for tick in range(len(stages) + POST_PARTS - 1):
        for p in range(POST_PARTS):
            if 0 <= tick - p < len(stages):
                stages[tick - p](p)


def _cast_kernel(w_ref, o_ref):
    o_ref[...] = w_ref[...].astype(o_ref.dtype)


def _to_bf16(w):
    layers, rows, cols = w.shape
    spec = pl.BlockSpec((None, rows // CAST_SPLIT, cols), lambda l, i: (l, i, 0))
    return pl.pallas_call(
        _cast_kernel,
        grid=(layers, CAST_SPLIT),
        in_specs=[spec],
        out_specs=spec,
        out_shape=jax.ShapeDtypeStruct(w.shape, BF16),
        compiler_params=pltpu.CompilerParams(
            dimension_semantics=("arbitrary", "arbitrary"), vmem_limit_bytes=VMEM_LIMIT),
        name="weights_to_bf16",
    )(w)


def _post_block(layer, ya, yb, x, wout, mg, mb, wg, wu, wd, fg, fb):
    n = x.shape[0]
    row_spec = lambda w: pl.BlockSpec((POST_TILE, w), lambda i: (i, 0))
    layer_spec = lambda w: pl.BlockSpec((None,) + w.shape[1:], lambda i: (layer, 0, 0),
                                        pipeline_mode=pl.Buffered(1))
    return pl.pallas_call(
        _post_kernel,
        grid=(n // POST_TILE,),
        in_specs=[row_spec(HALF), row_spec(HALF), row_spec(D_MODEL), _const_spec(wout.shape),
                  _const_spec(mg.shape), _const_spec(mb.shape), layer_spec(wg),
                  layer_spec(wu), layer_spec(wd), _const_spec(fg.shape),
                  _const_spec(fb.shape)],
        out_specs=row_spec(D_MODEL),
        out_shape=jax.ShapeDtypeStruct((n, D_MODEL), F32),
        scratch_shapes=[pltpu.VMEM((POST_TILE, D_MODEL), F32), pltpu.VMEM((POST_TILE, D_FF), BF16)],
        compiler_params=pltpu.CompilerParams(
            dimension_semantics=("arbitrary",), vmem_limit_bytes=VMEM_LIMIT),
        name="post_block",
    )(ya, yb, x, wout, mg, mb, wg, wu, wd, fg, fb)


def _gelu_tanh(x):
    return 0.5 * x * (1.0 + jnp.tanh(math.sqrt(2.0 / math.pi) * (x + 0.044715 * (x * x * x))))


def _dot_nt(a, b):
    return lax.dot_general(a, b, (((1,), (1,)), ((), ())), preferred_element_type=F32)


def _odd_proj_kernel(x_ref, w_in_ref, wqv_t_ref, lg_ref, lb_ref, sw_ref, sb_ref,
                     qt_ref, k_ref, vt_ref, yd_ref):
    rows = x_ref.shape[0]
    xb = x_ref[...].astype(BF16)

    def proj(c):
        return _dot(xb, w_in_ref[:, c * HALF:(c + 1) * HALF])

    u = _gelu_tanh(proj(3))
    vg = _gelu_tanh(proj(4))

    q_t = (_dot_nt(wqv_t_ref[0:HALF, :], xb) * (C_HEAD_DIM ** -0.5 * LOG2E)).astype(BF16)
    v_t = _dot_nt(wqv_t_ref[HALF:2 * HALF, :], xb).astype(BF16)
    for t in range(rows // TQ):
        qt_ref[t] = q_t[:, t * TQ:(t + 1) * TQ]
    for t in range(rows // TK):
        vt_ref[t] = v_t[:, t * TK:(t + 1) * TK]
    k_ref[...] = proj(1).astype(BF16)

    row = lax.broadcasted_iota(jnp.int32, (CHUNK, CHUNK), 0)
    col = lax.broadcasted_iota(jnp.int32, (CHUNK, CHUNK), 1)
    for g in range(D_GROUPS):
        lanes = slice(g * CHUNK, (g + 1) * CHUNK)
        vn = _layer_norm(vg[:, lanes], lg_ref[:, lanes], lb_ref[:, lanes]).astype(BF16)
        w_causal = jnp.where(col <= row, sw_ref[g], 0.0).astype(BF16)
        for r0 in range(0, rows, CHUNK):
            sp = _dot(w_causal, vn[r0:r0 + CHUNK, :]) + sb_ref[:, lanes]
            yd_ref[r0:r0 + CHUNK, lanes] = (u[r0:r0 + CHUNK, lanes] * sp).astype(BF16)


def _odd_proj(x, w_in, wqv_t, lg, lb, sw, sb):
    n = x.shape[0]
    row_spec = lambda w: pl.BlockSpec((TILE, w), lambda i: (i, 0))
    t_spec = lambda blk: pl.BlockSpec((TILE // blk, HALF, blk), lambda i: (i, 0, 0))
    t_shape = lambda blk: jax.ShapeDtypeStruct((n // blk, HALF, blk), BF16)
    r_shape = jax.ShapeDtypeStruct((n, HALF), BF16)
    return pl.pallas_call(
        _odd_proj_kernel,
        grid=(n // TILE,),
        in_specs=[row_spec(D_MODEL), _const_spec(w_in.shape), _const_spec(wqv_t.shape),
                  _const_spec(lg.shape), _const_spec(lb.shape), _const_spec(sw.shape),
                  _const_spec(sb.shape)],
        out_specs=[t_spec(TQ), row_spec(HALF), t_spec(TK), row_spec(HALF)],
        out_shape=[t_shape(TQ), r_shape, t_shape(TK), r_shape],
        compiler_params=pltpu.CompilerParams(
            dimension_semantics=("arbitrary",), vmem_limit_bytes=VMEM_LIMIT),
        name="odd_proj",
    )(x, w_in, wqv_t, lg, lb, sw, sb)


ONES_ROWS = 16


def _attn_kernel(qt_ref, k_ref, vt_ref, lq1_ref, lk1_ref, lq2_ref, lk2_ref, sg_ref, o_ref,
                 qs_scr, s_scr, cmax_scr, m_scr, acc_scr, *, lambda_init):
    qi = pl.program_id(1)
    heads = range(C_HEADS)
    head_cols = lambda h: slice(h * HEAD_W, (h + 1) * HEAD_W)

    chan = lax.broadcasted_iota(jnp.int32, (HEAD_W, TQ), 0)
    zero = jnp.zeros((HEAD_W, TQ), BF16)
    for h in heads:
        q_t = qt_ref[head_cols(h), :]
        qs_scr[h, :, 0:TQ] = jnp.where(chan < C_HEAD_DIM, q_t, zero)
        qs_scr[h, :, TQ:2 * TQ] = jnp.where(chan >= C_HEAD_DIM, q_t, zero)
    m_scr[...] = jnp.full(m_scr.shape, -jnp.inf, F32)
    acc_scr[...] = jnp.zeros(acc_scr.shape, F32)
    ones = jnp.ones((ONES_ROWS, TK), BF16)

    all_cols = (slice(0, 2 * TQ),)
    late_cols = (slice(TK, TQ), slice(TQ + TK, 2 * TQ))

    def gather(ref_row, cols):
        return ref_row if cols is all_cols else jnp.concatenate([ref_row[:, c] for c in cols], axis=1)

    def scores(t, slot, h, cols):
        rows = pl.ds(pl.multiple_of(t * TK, TK), TK)
        s = _dot(k_ref[rows, head_cols(h)], gather(qs_scr[h], cols))
        s_scr[slot, h, :, 0:s.shape[1]] = s
        if cols is all_cols:
            cmax_scr[slot, h] = jnp.max(s, axis=0, keepdims=True)

    def fold(t, slot, h, keep, cols):
        width = sum(c.stop - c.start for c in cols)
        s = s_scr[slot, h, :, 0:width]
        if keep is None:
            cmax = cmax_scr[slot, h]
        else:
            s = jnp.where(keep, s, -jnp.inf)
            cmax = jnp.max(s, axis=0, keepdims=True)
        m_old = gather(m_scr[h], cols)
        m_new = jnp.maximum(m_old, cmax)
        p = jnp.exp2(s - m_new).astype(BF16)
        v_ext = jnp.concatenate([vt_ref[t, head_cols(h), :], ones], axis=0)
        alpha = jnp.exp2(m_old - m_new)
        pv = _dot(v_ext, p)
        at = 0
        for c in cols:
            part = slice(at, at + c.stop - c.start)
            acc_scr[h, :, c] = alpha[:, part] * acc_scr[h, :, c] + pv[:, part]
            m_scr[h, :, c] = m_new[:, part]
            at = part.stop

    def step(t, slot, keep=None, cols=all_cols, next_cols=all_cols):
        for h in range(C_HEADS + 1):
            if next_cols is not None and h < C_HEADS:
                scores(t + 1, 1 - slot, h, next_cols)
            if h > 0:
                fold(t, slot, h - 1, keep, cols)

    for h in heads:
        scores(0, 0, h, all_cols)

    def pair(u, carry):
        step(2 * u, 0)
        step(2 * u + 1, 1)
        return carry

    lax.fori_loop(0, qi, pair, 0)

    key = lax.broadcasted_iota(jnp.int32, (TK, TQ), 0)
    qry = lax.broadcasted_iota(jnp.int32, (TK, TQ), 1)
    causal = key <= qry
    step(2 * qi, 0, keep=jnp.concatenate([causal] * 2, axis=1), next_cols=late_cols)
    step(2 * qi + 1, 1, keep=jnp.concatenate([causal[:, 0:TQ - TK]] * 2, axis=1),
         cols=late_cols, next_cols=None)

    lam = (jnp.exp(jnp.sum(lq1_ref[...] * lk1_ref[...], axis=-1, keepdims=True))
           - jnp.exp(jnp.sum(lq2_ref[...] * lk2_ref[...], axis=-1, keepdims=True))
           + lambda_init)
    for h in heads:
        o = acc_scr[h, 0:HEAD_W, :] / acc_scr[h, HEAD_W:HEAD_W + 1, :]
        o = o[:, 0:TQ] - lam * o[:, TQ:2 * TQ]
        o = o * lax.rsqrt(jnp.mean(o * o, axis=0, keepdims=True) + LN_EPS)
        o_ref[:, head_cols(h)] = (o.T * sg_ref[...] * (1.0 - lambda_init)).astype(BF16)


def _diff_attention(qt, k, vt, lq1, lk1, lq2, lk2, sg, batch, seq, lambda_init):
    n = k.shape[0]
    steps = seq // TQ
    qt_spec = pl.BlockSpec((None, HALF, TQ), lambda b, i: (b * steps + i, 0, 0))
    k_spec = pl.BlockSpec((seq, HALF), lambda b, i: (b, 0))
    vt_spec = pl.BlockSpec((seq // TK, HALF, TK), lambda b, i: (b, 0, 0))
    o_spec = pl.BlockSpec((TQ, HALF), lambda b, i: (b * steps + i, 0))
    return pl.pallas_call(
        functools.partial(_attn_kernel, lambda_init=lambda_init),
        grid=(batch, steps),
        in_specs=[qt_spec, k_spec, vt_spec, _const_spec(lq1.shape), _const_spec(lk1.shape),
                  _const_spec(lq2.shape), _const_spec(lk2.shape), _const_spec(sg.shape)],
        out_specs=o_spec,
        out_shape=jax.ShapeDtypeStruct((n, HALF), BF16),
        scratch_shapes=[pltpu.VMEM((C_HEADS, HEAD_W, 2 * TQ), BF16),
                        pltpu.VMEM((2, C_HEADS, TK, 2 * TQ), F32),
                        pltpu.VMEM((2, C_HEADS, 1, 2 * TQ), F32),
                        pltpu.VMEM((C_HEADS, 1, 2 * TQ), F32),
                        pltpu.VMEM((C_HEADS, HEAD_W + ONES_ROWS, 2 * TQ), F32)],
        compiler_params=pltpu.CompilerParams(
            dimension_semantics=("arbitrary", "arbitrary"), vmem_limit_bytes=VMEM_LIMIT),
        name="diff_attention",
    )(qt, k, vt, lq1, lk1, lq2, lk2, sg)


def kernel(x, even_w_in, even_conv_a_w, even_conv_b_w, even_conv_b_bias, even_conv_ln_g, even_conv_ln_b, even_w_out, odd_w_in, odd_lambda_q1, odd_lambda_k1, odd_lambda_q2, odd_lambda_k2, odd_subln_g, odd_gmlp_ln_g, odd_gmlp_ln_b, odd_spatial_w, odd_spatial_b, odd_w_out, mix_ln_g, mix_ln_b, ffn_w_gate, ffn_w_up, ffn_w_down, ffn_ln_g, ffn_ln_b):
    batch, seq, d = x.shape
    assert d == D_MODEL and seq % TILE == 0 and TILE % TQ == 0 and TQ == 2 * TK
    xf = x.reshape(batch * seq, d)
    row = lambda a: a.reshape(1, -1)
    bf = lambda a: a.astype(BF16)

    w_gate, w_up, w_down = _to_bf16(ffn_w_gate), _to_bf16(ffn_w_up), _to_bf16(ffn_w_down)

    def post(layer, ya, yb, xres, w_out):
        return _post_block(layer, ya, yb, xres, bf(w_out), row(mix_ln_g[layer]),
                           row(mix_ln_b[layer]), w_gate, w_up, w_down,
                           row(ffn_ln_g[layer]), row(ffn_ln_b[layer]))

    ya, yb = _even_mixer(xf, bf(even_w_in[0]), even_conv_a_w[0], even_conv_b_w[0],
                         row(even_conv_b_bias[0]), row(even_conv_ln_g[0]), row(even_conv_ln_b[0]),
                         batch, seq)
    xf = post(0, ya, yb, xf, even_w_out[0])

    lambda_init = 0.8 - 0.6 * math.exp(-0.3 * 1)
    sb = jnp.broadcast_to(odd_spatial_b[0].T[:, :, None], (CHUNK, D_GROUPS, CHUNK)).reshape(CHUNK, HALF)
    w_in = bf(odd_w_in[0])
    wqv_t = jnp.concatenate([w_in[:, 0:HALF], w_in[:, 2 * HALF:3 * HALF]], axis=1).T
    qt, k, vt, yd = _odd_proj(xf, w_in, wqv_t, row(odd_gmlp_ln_g[0]), row(odd_gmlp_ln_b[0]),
                              odd_spatial_w[0], sb)
    yc = _diff_attention(qt, k, vt, row(odd_lambda_q1[0]), row(odd_lambda_k1[0]),
                         row(odd_lambda_q2[0]), row(odd_lambda_k2[0]), row(odd_subln_g[0]),
                         batch, seq, lambda_init)
    xf = post(1, yc, yd, xf, odd_w_out[0])
    return xf.reshape(batch, seq, d)
```

```python
import functools
import math

import jax
import jax.numpy as jnp
from jax import lax
from jax.experimental import pallas as pl
from jax.experimental.pallas import tpu as pltpu

F32 = jnp.float32
BF16 = jnp.bfloat16

D_MODEL = 1024
HALF = D_MODEL // 2
A_CONV = 3
B_CONV = 31
C_HEAD_DIM = 64
C_HEADS = 4
HEAD_W = 2 * C_HEAD_DIM
CHUNK = 128
D_GROUPS = 4
D_FF = 2816
DEPTH = 2
ALPHA = (2 * DEPTH) ** 0.25
LN_EPS = 1e-5
LOG2E = 1.4426950408889634

SUBLANES = 8
HALO_A = 8
HALO_B = 32
CONV_ROWS = 32

TILE = 512
FF_CHUNK = 256
CAST_SPLIT = 2
POST_TILE = 1024
POST_PARTS = 4
TQ = 512
TK = 256
VMEM_LIMIT = 56 * 1024 * 1024


def _layer_norm(x, g, b):
    mu = jnp.mean(x, axis=-1, keepdims=True)
    xc = x - mu
    var = jnp.mean(xc * xc, axis=-1, keepdims=True)
    return xc * lax.rsqrt(var + LN_EPS) * g + b


def _sigmoid(x):
    return 1.0 / (1.0 + jnp.exp(-x))


def _dot(a, b):
    return jnp.dot(a, b, preferred_element_type=F32)


def _const_spec(shape):
    zeros = (0,) * len(shape)
    return pl.BlockSpec(shape, lambda *_: zeros, pipeline_mode=pl.Buffered(1))


def _causal_conv(ext_ref, shifted_ref, w_rows, first_row, rows, out_fn):
    def window(offset):
        r = offset % SUBLANES
        if shifted_ref is None or r == 0:
            return ext_ref[pl.ds(offset, CONV_ROWS), :]
        return shifted_ref[r - 1, pl.ds(offset - r, CONV_ROWS), :]

    for r0 in range(0, rows, CONV_ROWS):
        acc = w_rows[0] * window(first_row + r0)
        for k in range(1, len(w_rows)):
            acc = acc + w_rows[k] * window(first_row + r0 + k)
        out_fn(r0, acc)


def _replicated_taps(w_ref, wrep_ref, taps):
    for k in range(taps):
        wrep_ref[k] = jnp.broadcast_to(w_ref[k:k + 1, :], (SUBLANES, w_ref.shape[1]))
    return [jnp.concatenate([wrep_ref[k]] * (CONV_ROWS // SUBLANES), axis=0) for k in range(taps)]


def _even_mixer_kernel(x_ref, w_in_ref, wa_ref, wb_ref, bias_ref, g_ref, b_ref,
                       ya_ref, yb_ref, w_bf, aext, zext, zshift, warep, wbrep, gate_scr, conv_scr):
    rows = x_ref.shape[0]

    @pl.when((pl.program_id(0) == 0) & (pl.program_id(1) == 0))
    def _():
        w_bf[...] = w_in_ref[...].astype(BF16)

    @pl.when(pl.program_id(1) == 0)
    def _():
        aext[0:HALO_A, :] = jnp.zeros((HALO_A, HALF), F32)
        zext[0:HALO_B, :] = jnp.zeros((HALO_B, HALF), F32)

    xb = x_ref[...].astype(BF16)

    def proj(c):
        return _dot(xb, w_bf[:, c * HALF:(c + 1) * HALF])

    zext[HALO_B:HALO_B + rows, :] = proj(3) * _sigmoid(proj(4))
    gate_scr[...] = proj(0)
    aext[HALO_A:HALO_A + rows, :] = proj(1) * proj(2)

    for r in range(1, SUBLANES):
        zshift[r - 1] = zext[pl.ds(r, rows + HALO_B - SUBLANES), :]

    def store_b(r0, acc):
        conv_scr[pl.ds(r0, CONV_ROWS), :] = acc

    _causal_conv(zext, zshift, _replicated_taps(wb_ref, wbrep, B_CONV),
                 HALO_B - (B_CONV - 1), rows, store_b)
    zext[0:HALO_B, :] = zext[rows:rows + HALO_B, :]

    zn = _layer_norm(conv_scr[...] + bias_ref[...], g_ref[...], b_ref[...])
    yb_ref[...] = (zn * _sigmoid(zn)).astype(BF16)

    def store_a(r0, acc):
        ya_ref[pl.ds(r0, CONV_ROWS), :] = (gate_scr[pl.ds(r0, CONV_ROWS), :] * acc).astype(BF16)

    _causal_conv(aext, None, _replicated_taps(wa_ref, warep, A_CONV),
                 HALO_A - (A_CONV - 1), rows, store_a)
    aext[0:HALO_A, :] = aext[rows:rows + HALO_A, :]


def _even_mixer(x, w_in, wa, wb, bias, g, b, batch, seq):
    n = x.shape[0]
    steps = seq // TILE
    row_spec = lambda w: pl.BlockSpec((TILE, w), lambda bi, si: (bi * steps + si, 0))
    return pl.pallas_call(
        _even_mixer_kernel,
        grid=(batch, steps),
        in_specs=[row_spec(D_MODEL), _const_spec(w_in.shape), _const_spec(wa.shape),
                  _const_spec(wb.shape), _const_spec(bias.shape), _const_spec(g.shape),
                  _const_spec(b.shape)],
        out_specs=[row_spec(HALF), row_spec(HALF)],
        out_shape=[jax.ShapeDtypeStruct((n, HALF), BF16)] * 2,
        scratch_shapes=[pltpu.VMEM(w_in.shape, BF16),
                        pltpu.VMEM((TILE + HALO_A, HALF), F32),
                        pltpu.VMEM((TILE + HALO_B, HALF), F32),
                        pltpu.VMEM((SUBLANES - 1, TILE + HALO_B - SUBLANES, HALF), F32),
                        pltpu.VMEM((A_CONV, SUBLANES, HALF), F32),
                        pltpu.VMEM((B_CONV, SUBLANES, HALF), F32),
                        pltpu.VMEM((TILE, HALF), F32),
                        pltpu.VMEM((TILE, HALF), F32)],
        compiler_params=pltpu.CompilerParams(
            dimension_semantics=("arbitrary", "arbitrary"), vmem_limit_bytes=VMEM_LIMIT),
        name="even_mixer",
    )(x, w_in, wa, wb, bias, g, b)


def _post_kernel(ya_ref, yb_ref, x_ref, wout_ref, mg_ref, mb_ref, wg_ref, wu_ref, wd_ref,
                 fg_ref, fb_ref, o_ref, x1_scr, h_scr):
    rows = x_ref.shape[0]
    parts = [slice(r, r + rows // POST_PARTS) for r in range(0, rows, rows // POST_PARTS)]
    carried = {}

    def out_proj(p):
        rws = parts[p]
        carried[p] = (_dot(ya_ref[rws, :], wout_ref[0:HALF, :])
                      + _dot(yb_ref[rws, :], wout_ref[HALF:D_MODEL, :]))

    def mix_norm(p):
        rws = parts[p]
        x1_scr[rws, :] = _layer_norm(ALPHA * x_ref[rws, :] + carried.pop(p), mg_ref[...], mb_ref[...])

    def gate_up(p):
        rws = parts[p]
        xb = x1_scr[rws, :].astype(BF16)
        for c in range(0, D_FF, FF_CHUNK):
            gate = _dot(xb, wg_ref[:, c:c + FF_CHUNK])
            up = _dot(xb, wu_ref[:, c:c + FF_CHUNK])
            h_scr[rws, c:c + FF_CHUNK] = (gate * _sigmoid(gate) * up).astype(BF16)

    def down_proj(p):
        carried[p] = _dot(h_scr[parts[p], :], wd_ref[...])

    def ffn_norm(p):
        rws = parts[p]
        o_ref[rws, :] = _layer_norm(ALPHA * x1_scr[rws, :] + carried.pop(p), fg_ref[...], fb_ref[...])

    stages = (out_proj, mix_norm, gate_up, down_proj, ffn_norm)
    for tick in range(len(stages) + POST_PARTS - 1):
        for p in range(POST_PARTS):
            if 0 <= tick - p < len(stages):
                stages[tick - p](p)


def _cast_kernel(w_ref, o_ref):
    o_ref[...] = w_ref[...].astype(o_ref.dtype)


def _to_bf16(w):
    layers, rows, cols = w.shape
    spec = pl.BlockSpec((None, rows // CAST_SPLIT, cols), lambda l, i: (l, i, 0))
    return pl.pallas_call(
        _cast_kernel,
        grid=(layers, CAST_SPLIT),
        in_specs=[spec],
        out_specs=spec,
        out_shape=jax.ShapeDtypeStruct(w.shape, BF16),
        compiler_params=pltpu.CompilerParams(
            dimension_semantics=("arbitrary", "arbitrary"), vmem_limit_bytes=VMEM_LIMIT),
        name="weights_to_bf16",
    )(w)


def _post_block(layer, ya, yb, x, wout, mg, mb, wg, wu, wd, fg, fb):
    n = x.shape[0]
    row_spec = lambda w: pl.BlockSpec((POST_TILE, w), lambda i: (i, 0))
    layer_spec = lambda w: pl.BlockSpec((None,) + w.shape[1:], lambda i: (layer, 0, 0),
                                        pipeline_mode=pl.Buffered(1))
    return pl.pallas_call(
        _post_kernel,
        grid=(n // POST_TILE,),
        in_specs=[row_spec(HALF), row_spec(HALF), row_spec(D_MODEL), _const_spec(wout.shape),
                  _const_spec(mg.shape), _const_spec(mb.shape), layer_spec(wg),
                  layer_spec(wu), layer_spec(wd), _const_spec(fg.shape),
                  _const_spec(fb.shape)],
        out_specs=row_spec(D_MODEL),
        out_shape=jax.ShapeDtypeStruct((n, D_MODEL), F32),
        scratch_shapes=[pltpu.VMEM((POST_TILE, D_MODEL), F32), pltpu.VMEM((POST_TILE, D_FF), BF16)],
        compiler_params=pltpu.CompilerParams(
            dimension_semantics=("arbitrary",), vmem_limit_bytes=VMEM_LIMIT),
        name="post_block",
    )(ya, yb, x, wout, mg, mb, wg, wu, wd, fg, fb)


def _gelu_tanh(x):
    return 0.5 * x * (1.0 + jnp.tanh(math.sqrt(2.0 / math.pi) * (x + 0.044715 * (x * x * x))))


def _dot_nt(a, b):
    return lax.dot_general(a, b, (((1,), (1,)), ((), ())), preferred_element_type=F32)


def _odd_proj_kernel(x_ref, w_in_ref, lg_ref, lb_ref, sw_ref, sb_ref,
                     qt_ref, k_ref, vt_ref, yd_ref, w_bf, wqv_t):
    rows = x_ref.shape[0]

    @pl.when(pl.program_id(0) == 0)
    def _():
        w_bf[...] = w_in_ref[...].astype(BF16)
        wqv_t[0:HALF, :] = w_in_ref[:, 0:HALF].T.astype(BF16)
        wqv_t[HALF:2 * HALF, :] = w_in_ref[:, 2 * HALF:3 * HALF].T.astype(BF16)

    xb = x_ref[...].astype(BF16)

    def proj(c):
        return _dot(xb, w_bf[:, c * HALF:(c + 1) * HALF])

    u = _gelu_tanh(proj(3))
    vg = _gelu_tanh(proj(4))

    q_t = (_dot_nt(wqv_t[0:HALF, :], xb) * (C_HEAD_DIM ** -0.5 * LOG2E)).astype(BF16)
    v_t = _dot_nt(wqv_t[HALF:2 * HALF, :], xb).astype(BF16)
    for t in range(rows // TQ):
        qt_ref[t] = q_t[:, t * TQ:(t + 1) * TQ]
    for t in range(rows // TK):
        vt_ref[t] = v_t[:, t * TK:(t + 1) * TK]
    k_ref[...] = proj(1).astype(BF16)

    row = lax.broadcasted_iota(jnp.int32, (CHUNK, CHUNK), 0)
    col = lax.broadcasted_iota(jnp.int32, (CHUNK, CHUNK), 1)
    for g in range(D_GROUPS):
        lanes = slice(g * CHUNK, (g + 1) * CHUNK)
        vn = _layer_norm(vg[:, lanes], lg_ref[:, lanes], lb_ref[:, lanes]).astype(BF16)
        w_causal = jnp.where(col <= row, sw_ref[g], 0.0).astype(BF16)
        for r0 in range(0, rows, CHUNK):
            sp = _dot(w_causal, vn[r0:r0 + CHUNK, :]) + sb_ref[:, lanes]
            yd_ref[r0:r0 + CHUNK, lanes] = (u[r0:r0 + CHUNK, lanes] * sp).astype(BF16)


def _odd_proj(x, w_in, lg, lb, sw, sb):
    n = x.shape[0]
    row_spec = lambda w: pl.BlockSpec((TILE, w), lambda i: (i, 0))
    t_spec = lambda blk: pl.BlockSpec((TILE // blk, HALF, blk), lambda i: (i, 0, 0))
    t_shape = lambda blk: jax.ShapeDtypeStruct((n // blk, HALF, blk), BF16)
    r_shape = jax.ShapeDtypeStruct((n, HALF), BF16)
    return pl.pallas_call(
        _odd_proj_kernel,
        grid=(n // TILE,),
        in_specs=[row_spec(D_MODEL), _const_spec(w_in.shape),
                  _const_spec(lg.shape), _const_spec(lb.shape), _const_spec(sw.shape),
                  _const_spec(sb.shape)],
        out_specs=[t_spec(TQ), row_spec(HALF), t_spec(TK), row_spec(HALF)],
        out_shape=[t_shape(TQ), r_shape, t_shape(TK), r_shape],
        scratch_shapes=[pltpu.VMEM(w_in.shape, BF16), pltpu.VMEM((2 * HALF, D_MODEL), BF16)],
        compiler_params=pltpu.CompilerParams(
            dimension_semantics=("arbitrary",), vmem_limit_bytes=VMEM_LIMIT),
        name="odd_proj",
    )(x, w_in, lg, lb, sw, sb)


ONES_ROWS = 16


def _attn_kernel(qt_ref, k_ref, vt_ref, lq1_ref, lk1_ref, lq2_ref, lk2_ref, sg_ref, o_ref,
                 qs_scr, s_scr, cmax_scr, m_scr, acc_scr, *, lambda_init):
    qi = pl.program_id(1)
    heads = range(C_HEADS)
    head_cols = lambda h: slice(h * HEAD_W, (h + 1) * HEAD_W)

    chan = lax.broadcasted_iota(jnp.int32, (HEAD_W, TQ), 0)
    zero = jnp.zeros((HEAD_W, TQ), BF16)
    for h in heads:
        q_t = qt_ref[head_cols(h), :]
        qs_scr[h, :, 0:TQ] = jnp.where(chan < C_HEAD_DIM, q_t, zero)
        qs_scr[h, :, TQ:2 * TQ] = jnp.where(chan >= C_HEAD_DIM, q_t, zero)
    m_scr[...] = jnp.full(m_scr.shape, -jnp.inf, F32)
    acc_scr[...] = jnp.zeros(acc_scr.shape, F32)
    ones = jnp.ones((ONES_ROWS, TK), BF16)

    all_cols = (slice(0, 2 * TQ),)
    late_cols = (slice(TK, TQ), slice(TQ + TK, 2 * TQ))

    def gather(ref_row, cols):
        return ref_row if cols is all_cols else jnp.concatenate([ref_row[:, c] for c in cols], axis=1)

    def scores(t, slot, h, cols):
        rows = pl.ds(pl.multiple_of(t * TK, TK), TK)
        s = _dot(k_ref[rows, head_cols(h)], gather(qs_scr[h], cols))
        s_scr[slot, h, :, 0:s.shape[1]] = s
        if cols is all_cols:
            cmax_scr[slot, h] = jnp.max(s, axis=0, keepdims=True)

    def fold(t, slot, h, keep, cols):
        width = sum(c.stop - c.start for c in cols)
        s = s_scr[slot, h, :, 0:width]
        if keep is None:
            cmax = cmax_scr[slot, h]
        else:
            s = jnp.where(keep, s, -jnp.inf)
            cmax = jnp.max(s, axis=0, keepdims=True)
        m_old = gather(m_scr[h], cols)
        m_new = jnp.maximum(m_old, cmax)
        p = jnp.exp2(s - m_new).astype(BF16)
        v_ext = jnp.concatenate([vt_ref[t, head_cols(h), :], ones], axis=0)
        alpha = jnp.exp2(m_old - m_new)
        pv = _dot(v_ext, p)
        at = 0
        for c in cols:
            part = slice(at, at + c.stop - c.start)
            acc_scr[h, :, c] = alpha[:, part] * acc_scr[h, :, c] + pv[:, part]
            m_scr[h, :, c] = m_new[:, part]
            at = part.stop

    def step(t, slot, keep=None, cols=all_cols, next_cols=all_cols):
        for h in range(C_HEADS + 1):
            if next_cols is not None and h < C_HEADS:
                scores(t + 1, 1 - slot, h, next_cols)
            if h > 0:
                fold(t, slot, h - 1, keep, cols)

    for h in heads:
        scores(0, 0, h, all_cols)

    def pair(u, carry):
        step(2 * u, 0)
        step(2 * u + 1, 1)
        return carry

    lax.fori_loop(0, qi, pair, 0)

    key = lax.broadcasted_iota(jnp.int32, (TK, TQ), 0)
    qry = lax.broadcasted_iota(jnp.int32, (TK, TQ), 1)
    causal = key <= qry
    step(2 * qi, 0, keep=jnp.concatenate([causal] * 2, axis=1), next_cols=late_cols)
    step(2 * qi + 1, 1, keep=jnp.concatenate([causal[:, 0:TQ - TK]] * 2, axis=1),
         cols=late_cols, next_cols=None)

    lam = (jnp.exp(jnp.sum(lq1_ref[...] * lk1_ref[...], axis=-1, keepdims=True))
           - jnp.exp(jnp.sum(lq2_ref[...] * lk2_ref[...], axis=-1, keepdims=True))
           + lambda_init)
    for h in heads:
        o = acc_scr[h, 0:HEAD_W, :] / acc_scr[h, HEAD_W:HEAD_W + 1, :]
        o = o[:, 0:TQ] - lam * o[:, TQ:2 * TQ]
        o = o * lax.rsqrt(jnp.mean(o * o, axis=0, keepdims=True) + LN_EPS)
        o_ref[:, head_cols(h)] = (o.T * sg_ref[...] * (1.0 - lambda_init)).astype(BF16)


def _diff_attention(qt, k, vt, lq1, lk1, lq2, lk2, sg, batch, seq, lambda_init):
    n = k.shape[0]
    steps = seq // TQ
    qt_spec = pl.BlockSpec((None, HALF, TQ), lambda b, i: (b * steps + i, 0, 0))
    k_spec = pl.BlockSpec((seq, HALF), lambda b, i: (b, 0))
    vt_spec = pl.BlockSpec((seq // TK, HALF, TK), lambda b, i: (b, 0, 0))
    o_spec = pl.BlockSpec((TQ, HALF), lambda b, i: (b * steps + i, 0))
    return pl.pallas_call(
        functools.partial(_attn_kernel, lambda_init=lambda_init),
        grid=(batch, steps),
        in_specs=[qt_spec, k_spec, vt_spec, _const_spec(lq1.shape), _const_spec(lk1.shape),
                  _const_spec(lq2.shape), _const_spec(lk2.shape), _const_spec(sg.shape)],
        out_specs=o_spec,
        out_shape=jax.ShapeDtypeStruct((n, HALF), BF16),
        scratch_shapes=[pltpu.VMEM((C_HEADS, HEAD_W, 2 * TQ), BF16),
                        pltpu.VMEM((2, C_HEADS, TK, 2 * TQ), F32),
                        pltpu.VMEM((2, C_HEADS, 1, 2 * TQ), F32),
                        pltpu.VMEM((C_HEADS, 1, 2 * TQ), F32),
                        pltpu.VMEM((C_HEADS, HEAD_W + ONES_ROWS, 2 * TQ), F32)],
        compiler_params=pltpu.CompilerParams(
            dimension_semantics=("arbitrary", "arbitrary"), vmem_limit_bytes=VMEM_LIMIT),
        name="diff_attention",
    )(qt, k, vt, lq1, lk1, lq2, lk2, sg)


def kernel(x, even_w_in, even_conv_a_w, even_conv_b_w, even_conv_b_bias, even_conv_ln_g, even_conv_ln_b, even_w_out, odd_w_in, odd_lambda_q1, odd_lambda_k1, odd_lambda_q2, odd_lambda_k2, odd_subln_g, odd_gmlp_ln_g, odd_gmlp_ln_b, odd_spatial_w, odd_spatial_b, odd_w_out, mix_ln_g, mix_ln_b, ffn_w_gate, ffn_w_up, ffn_w_down, ffn_ln_g, ffn_ln_b):
    batch, seq, d = x.shape
    assert d == D_MODEL and seq % TILE == 0 and TILE % TQ == 0 and TQ == 2 * TK
    xf = x.reshape(batch * seq, d)
    row = lambda a: a.reshape(1, -1)
    bf = lambda a: a.astype(BF16)

    w_gate, w_up, w_down = _to_bf16(ffn_w_gate), _to_bf16(ffn_w_up), _to_bf16(ffn_w_down)

    def post(layer, ya, yb, xres, w_out):
        return _post_block(layer, ya, yb, xres, bf(w_out), row(mix_ln_g[layer]),
                           row(mix_ln_b[layer]), w_gate, w_up, w_down,
                           row(ffn_ln_g[layer]), row(ffn_ln_b[layer]))

    ya, yb = _even_mixer(xf, even_w_in[0], even_conv_a_w[0], even_conv_b_w[0],
                         row(even_conv_b_bias[0]), row(even_conv_ln_g[0]), row(even_conv_ln_b[0]),
                         batch, seq)
    xf = post(0, ya, yb, xf, even_w_out[0])

    lambda_init = 0.8 - 0.6 * math.exp(-0.3 * 1)
    sb = jnp.broadcast_to(odd_spatial_b[0].T[:, :, None], (CHUNK, D_GROUPS, CHUNK)).reshape(CHUNK, HALF)
    qt, k, vt, yd = _odd_proj(xf, odd_w_in[0], row(odd_gmlp_ln_g[0]), row(odd_gmlp_ln_b[0]),
                              odd_spatial_w[0], sb)
    yc = _diff_attention(qt, k, vt, row(odd_lambda_q1[0]), row(odd_lambda_k1[0]),
                         row(odd_lambda_q2[0]), row(odd_lambda_k2[0]), row(odd_subln_g[0]),
                         batch, seq, lambda_init)
    xf = post(1, yc, yd, xf, odd_w_out[0])
    return xf.reshape(batch, seq, d)
```

```python
import functools
import math

import jax
import jax.numpy as jnp
from jax import lax
from jax.experimental import pallas as pl
from jax.experimental.pallas import tpu as pltpu

F32 = jnp.float32
BF16 = jnp.bfloat16

D_MODEL = 1024
HALF = D_MODEL // 2
A_CONV = 3
B_CONV = 31
C_HEAD_DIM = 64
C_HEADS = 4
HEAD_W = 2 * C_HEAD_DIM
CHUNK = 128
D_GROUPS = 4
D_FF = 2816
DEPTH = 2
ALPHA = (2 * DEPTH) ** 0.25
LN_EPS = 1e-5
LOG2E = 1.4426950408889634

SUBLANES = 8
HALO_A = 8
HALO_B = 32
CONV_ROWS = 32

TILE = 512
FF_CHUNK = 256
CAST_SPLIT = 2
POST_TILE = 1024
ODD_TILE = 1024
POST_PARTS = 4
TQ = 512
TK = 256
VMEM_LIMIT = 56 * 1024 * 1024


def _layer_norm(x, g, b):
    mu = jnp.mean(x, axis=-1, keepdims=True)
    xc = x - mu
    var = jnp.mean(xc * xc, axis=-1, keepdims=True)
    return xc * lax.rsqrt(var + LN_EPS) * g + b


def _sigmoid(x):
    return 1.0 / (1.0 + jnp.exp(-x))


def _dot(a, b):
    return jnp.dot(a, b, preferred_element_type=F32)


def _const_spec(shape):
    zeros = (0,) * len(shape)
    return pl.BlockSpec(shape, lambda *_: zeros, pipeline_mode=pl.Buffered(1))


def _causal_conv(ext_ref, shifted_ref, w_rows, first_row, rows, out_fn):
    def window(offset):
        r = offset % SUBLANES
        if shifted_ref is None or r == 0:
            return ext_ref[pl.ds(offset, CONV_ROWS), :]
        return shifted_ref[r - 1, pl.ds(offset - r, CONV_ROWS), :]

    for r0 in range(0, rows, CONV_ROWS):
        acc = w_rows[0] * window(first_row + r0)
        for k in range(1, len(w_rows)):
            acc = acc + w_rows[k] * window(first_row + r0 + k)
        out_fn(r0, acc)


def _replicated_taps(w_ref, wrep_ref, taps):
    for k in range(taps):
        wrep_ref[k] = jnp.broadcast_to(w_ref[k:k + 1, :], (SUBLANES, w_ref.shape[1]))
    return [jnp.concatenate([wrep_ref[k]] * (CONV_ROWS // SUBLANES), axis=0) for k in range(taps)]


def _even_mixer_kernel(x_ref, w_in_ref, wa_ref, wb_ref, bias_ref, g_ref, b_ref,
                       ya_ref, yb_ref, w_bf, aext, zext, zshift, warep, wbrep, gate_scr, conv_scr):
    rows = x_ref.shape[0]

    @pl.when((pl.program_id(0) == 0) & (pl.program_id(1) == 0))
    def _():
        w_bf[...] = w_in_ref[...].astype(BF16)

    @pl.when(pl.program_id(1) == 0)
    def _():
        aext[0:HALO_A, :] = jnp.zeros((HALO_A, HALF), F32)
        zext[0:HALO_B, :] = jnp.zeros((HALO_B, HALF), F32)

    xb = x_ref[...].astype(BF16)

    def proj(c):
        return _dot(xb, w_bf[:, c * HALF:(c + 1) * HALF])

    zext[HALO_B:HALO_B + rows, :] = proj(3) * _sigmoid(proj(4))
    gate_scr[...] = proj(0)
    aext[HALO_A:HALO_A + rows, :] = proj(1) * proj(2)

    for r in range(1, SUBLANES):
        zshift[r - 1] = zext[pl.ds(r, rows + HALO_B - SUBLANES), :]

    def store_b(r0, acc):
        conv_scr[pl.ds(r0, CONV_ROWS), :] = acc

    _causal_conv(zext, zshift, _replicated_taps(wb_ref, wbrep, B_CONV),
                 HALO_B - (B_CONV - 1), rows, store_b)
    zext[0:HALO_B, :] = zext[rows:rows + HALO_B, :]

    zn = _layer_norm(conv_scr[...] + bias_ref[...], g_ref[...], b_ref[...])
    yb_ref[...] = (zn * _sigmoid(zn)).astype(BF16)

    def store_a(r0, acc):
        ya_ref[pl.ds(r0, CONV_ROWS), :] = (gate_scr[pl.ds(r0, CONV_ROWS), :] * acc).astype(BF16)

    _causal_conv(aext, None, _replicated_taps(wa_ref, warep, A_CONV),
                 HALO_A - (A_CONV - 1), rows, store_a)
    aext[0:HALO_A, :] = aext[rows:rows + HALO_A, :]


def _even_mixer(x, w_in, wa, wb, bias, g, b, batch, seq):
    n = x.shape[0]
    steps = seq // TILE
    row_spec = lambda w: pl.BlockSpec((TILE, w), lambda bi, si: (bi * steps + si, 0))
    return pl.pallas_call(
        _even_mixer_kernel,
        grid=(batch, steps),
        in_specs=[row_spec(D_MODEL), _const_spec(w_in.shape), _const_spec(wa.shape),
                  _const_spec(wb.shape), _const_spec(bias.shape), _const_spec(g.shape),
                  _const_spec(b.shape)],
        out_specs=[row_spec(HALF), row_spec(HALF)],
        out_shape=[jax.ShapeDtypeStruct((n, HALF), BF16)] * 2,
        scratch_shapes=[pltpu.VMEM(w_in.shape, BF16),
                        pltpu.VMEM((TILE + HALO_A, HALF), F32),
                        pltpu.VMEM((TILE + HALO_B, HALF), F32),
                        pltpu.VMEM((SUBLANES - 1, TILE + HALO_B - SUBLANES, HALF), F32),
                        pltpu.VMEM((A_CONV, SUBLANES, HALF), F32),
                        pltpu.VMEM((B_CONV, SUBLANES, HALF), F32),
                        pltpu.VMEM((TILE, HALF), F32),
                        pltpu.VMEM((TILE, HALF), F32)],
        compiler_params=pltpu.CompilerParams(
            dimension_semantics=("arbitrary", "arbitrary"), vmem_limit_bytes=VMEM_LIMIT),
        name="even_mixer",
    )(x, w_in, wa, wb, bias, g, b)


def _post_kernel(ya_ref, yb_ref, x_ref, wout_ref, mg_ref, mb_ref, wg_ref, wu_ref, wd_ref,
                 fg_ref, fb_ref, o_ref, x1_scr, h_scr):
    rows = x_ref.shape[0]
    parts = [slice(r, r + rows // POST_PARTS) for r in range(0, rows, rows // POST_PARTS)]
    carried = {}

    def out_proj(p):
        rws = parts[p]
        carried[p] = (_dot(ya_ref[rws, :], wout_ref[0:HALF, :])
                      + _dot(yb_ref[rws, :], wout_ref[HALF:D_MODEL, :]))

    def mix_norm(p):
        rws = parts[p]
        x1_scr[rws, :] = _layer_norm(ALPHA * x_ref[rws, :] + carried.pop(p), mg_ref[...], mb_ref[...])

    def gate_up(p):
        rws = parts[p]
        xb = x1_scr[rws, :].astype(BF16)
        for c in range(0, D_FF, FF_CHUNK):
            gate = _dot(xb, wg_ref[:, c:c + FF_CHUNK])
            up = _dot(xb, wu_ref[:, c:c + FF_CHUNK])
            h_scr[rws, c:c + FF_CHUNK] = (gate * _sigmoid(gate) * up).astype(BF16)

    def down_proj(p):
        carried[p] = _dot(h_scr[parts[p], :], wd_ref[...])

    def ffn_norm(p):
        rws = parts[p]
        o_ref[rws, :] = _layer_norm(ALPHA * x1_scr[rws, :] + carried.pop(p), fg_ref[...], fb_ref[...])

    stages = (out_proj, mix_norm, gate_up, down_proj, ffn_norm)
    for tick in range(len(stages) + POST_PARTS - 1):
        for p in reversed(range(POST_PARTS)):
            if 0 <= tick - p < len(stages):
                stages[tick - p](p)


def _cast_kernel(w_ref, o_ref):
    o_ref[...] = w_ref[...].astype(o_ref.dtype)


def _to_bf16(w):
    layers, rows, cols = w.shape
    spec = pl.BlockSpec((None, rows // CAST_SPLIT, cols), lambda l, i: (l, i, 0))
    return pl.pallas_call(
        _cast_kernel,
        grid=(layers, CAST_SPLIT),
        in_specs=[spec],
        out_specs=spec,
        out_shape=jax.ShapeDtypeStruct(w.shape, BF16),
        compiler_params=pltpu.CompilerParams(
            dimension_semantics=("arbitrary", "arbitrary"), vmem_limit_bytes=VMEM_LIMIT),
        name="weights_to_bf16",
    )(w)


def _post_block(layer, ya, yb, x, wout, mg, mb, wg, wu, wd, fg, fb):
    n = x.shape[0]
    row_spec = lambda w: pl.BlockSpec((POST_TILE, w), lambda i: (i, 0))
    layer_spec = lambda w: pl.BlockSpec((None,) + w.shape[1:], lambda i: (layer, 0, 0),
                                        pipeline_mode=pl.Buffered(1))
    return pl.pallas_call(
        _post_kernel,
        grid=(n // POST_TILE,),
        in_specs=[row_spec(HALF), row_spec(HALF), row_spec(D_MODEL), _const_spec(wout.shape),
                  _const_spec(mg.shape), _const_spec(mb.shape), layer_spec(wg),
                  layer_spec(wu), layer_spec(wd), _const_spec(fg.shape),
                  _const_spec(fb.shape)],
        out_specs=row_spec(D_MODEL),
        out_shape=jax.ShapeDtypeStruct((n, D_MODEL), F32),
        scratch_shapes=[pltpu.VMEM((POST_TILE, D_MODEL), F32), pltpu.VMEM((POST_TILE, D_FF), BF16)],
        compiler_params=pltpu.CompilerParams(
            dimension_semantics=("arbitrary",), vmem_limit_bytes=VMEM_LIMIT),
        name="post_block",
    )(ya, yb, x, wout, mg, mb, wg, wu, wd, fg, fb)


def _gelu_tanh(x):
    return 0.5 * x * (1.0 + jnp.tanh(math.sqrt(2.0 / math.pi) * (x + 0.044715 * (x * x * x))))


def _dot_nt(a, b):
    return lax.dot_general(a, b, (((1,), (1,)), ((), ())), preferred_element_type=F32)


def _odd_proj_kernel(x_ref, w_in_ref, lg_ref, lb_ref, sw_ref, sb_ref,
                     qt_ref, k_ref, vt_ref, yd_ref, w_bf, wqv_t):
    rows = x_ref.shape[0]

    @pl.when(pl.program_id(0) == 0)
    def _():
        w_bf[...] = w_in_ref[...].astype(BF16)
        wqv_t[0:HALF, :] = w_in_ref[:, 0:HALF].T.astype(BF16)
        wqv_t[HALF:2 * HALF, :] = w_in_ref[:, 2 * HALF:3 * HALF].T.astype(BF16)

    xb = x_ref[...].astype(BF16)

    def proj(c):
        return _dot(xb, w_bf[:, c * HALF:(c + 1) * HALF])

    u = _gelu_tanh(proj(3))
    vg = _gelu_tanh(proj(4))

    q_t = (_dot_nt(wqv_t[0:HALF, :], xb) * (C_HEAD_DIM ** -0.5 * LOG2E)).astype(BF16)
    v_t = _dot_nt(wqv_t[HALF:2 * HALF, :], xb).astype(BF16)
    for t in range(rows // TQ):
        qt_ref[t] = q_t[:, t * TQ:(t + 1) * TQ]
    for t in range(rows // TK):
        vt_ref[t] = v_t[:, t * TK:(t + 1) * TK]
    k_ref[...] = proj(1).astype(BF16)

    row = lax.broadcasted_iota(jnp.int32, (CHUNK, CHUNK), 0)
    col = lax.broadcasted_iota(jnp.int32, (CHUNK, CHUNK), 1)
    for g in range(D_GROUPS):
        lanes = slice(g * CHUNK, (g + 1) * CHUNK)
        vn = _layer_norm(vg[:, lanes], lg_ref[:, lanes], lb_ref[:, lanes]).astype(BF16)
        w_causal = jnp.where(col <= row, sw_ref[g], 0.0).astype(BF16)
        for r0 in range(0, rows, CHUNK):
            sp = _dot(w_causal, vn[r0:r0 + CHUNK, :]) + sb_ref[:, lanes]
            yd_ref[r0:r0 + CHUNK, lanes] = (u[r0:r0 + CHUNK, lanes] * sp).astype(BF16)


def _odd_proj(x, w_in, lg, lb, sw, sb):
    n = x.shape[0]
    row_spec = lambda w: pl.BlockSpec((ODD_TILE, w), lambda i: (i, 0))
    t_spec = lambda blk: pl.BlockSpec((ODD_TILE // blk, HALF, blk), lambda i: (i, 0, 0))
    t_shape = lambda blk: jax.ShapeDtypeStruct((n // blk, HALF, blk), BF16)
    r_shape = jax.ShapeDtypeStruct((n, HALF), BF16)
    return pl.pallas_call(
        _odd_proj_kernel,
        grid=(n // ODD_TILE,),
        in_specs=[row_spec(D_MODEL), _const_spec(w_in.shape),
                  _const_spec(lg.shape), _const_spec(lb.shape), _const_spec(sw.shape),
                  _const_spec(sb.shape)],
        out_specs=[t_spec(TQ), row_spec(HALF), t_spec(TK), row_spec(HALF)],
        out_shape=[t_shape(TQ), r_shape, t_shape(TK), r_shape],
        scratch_shapes=[pltpu.VMEM(w_in.shape, BF16), pltpu.VMEM((2 * HALF, D_MODEL), BF16)],
        compiler_params=pltpu.CompilerParams(
            dimension_semantics=("arbitrary",), vmem_limit_bytes=VMEM_LIMIT),
        name="odd_proj",
    )(x, w_in, lg, lb, sw, sb)


ONES_ROWS = 16


def _attn_kernel(qt_ref, k_ref, vt_ref, lq1_ref, lk1_ref, lq2_ref, lk2_ref, sg_ref, o_ref,
                 qs_scr, s_scr, cmax_scr, m_scr, acc_scr, *, lambda_init):
    qi = pl.program_id(1)
    heads = range(C_HEADS)
    head_cols = lambda h: slice(h * HEAD_W, (h + 1) * HEAD_W)

    chan = lax.broadcasted_iota(jnp.int32, (HEAD_W, TQ), 0)
    zero = jnp.zeros((HEAD_W, TQ), BF16)
    for h in heads:
        q_t = qt_ref[head_cols(h), :]
        qs_scr[h, :, 0:TQ] = jnp.where(chan < C_HEAD_DIM, q_t, zero)
        qs_scr[h, :, TQ:2 * TQ] = jnp.where(chan >= C_HEAD_DIM, q_t, zero)
    m_scr[...] = jnp.full(m_scr.shape, -jnp.inf, F32)
    acc_scr[...] = jnp.zeros(acc_scr.shape, F32)
    ones = jnp.ones((ONES_ROWS, TK), BF16)

    all_cols = (slice(0, 2 * TQ),)
    late_cols = (slice(TK, TQ), slice(TQ + TK, 2 * TQ))

    def gather(ref_row, cols):
        return ref_row if cols is all_cols else jnp.concatenate([ref_row[:, c] for c in cols], axis=1)

    def scores(t, slot, h, cols):
        rows = pl.ds(pl.multiple_of(t * TK, TK), TK)
        s = _dot(k_ref[rows, head_cols(h)], gather(qs_scr[h], cols))
        s_scr[slot, h, :, 0:s.shape[1]] = s
        if cols is all_cols:
            cmax_scr[slot, h] = jnp.max(s, axis=0, keepdims=True)

    def fold(t, slot, h, keep, cols):
        width = sum(c.stop - c.start for c in cols)
        s = s_scr[slot, h, :, 0:width]
        if keep is None:
            cmax = cmax_scr[slot, h]
        else:
            s = jnp.where(keep, s, -jnp.inf)
            cmax = jnp.max(s, axis=0, keepdims=True)
        m_old = gather(m_scr[h], cols)
        m_new = jnp.maximum(m_old, cmax)
        p = jnp.exp2(s - m_new).astype(BF16)
        v_ext = jnp.concatenate([vt_ref[t, head_cols(h), :], ones], axis=0)
        alpha = jnp.exp2(m_old - m_new)
        pv = _dot(v_ext, p)
        at = 0
        for c in cols:
            part = slice(at, at + c.stop - c.start)
            acc_scr[h, :, c] = alpha[:, part] * acc_scr[h, :, c] + pv[:, part]
            m_scr[h, :, c] = m_new[:, part]
            at = part.stop

    def step(t, slot, keep=None, cols=all_cols, next_cols=all_cols):
        for h in range(C_HEADS + 1):
            if next_cols is not None and h < C_HEADS:
                scores(t + 1, 1 - slot, h, next_cols)
            if h > 0:
                fold(t, slot, h - 1, keep, cols)

    for h in heads:
        scores(0, 0, h, all_cols)

    def pair(u, carry):
        step(2 * u, 0)
        step(2 * u + 1, 1)
        return carry

    lax.fori_loop(0, qi, pair, 0)

    key = lax.broadcasted_iota(jnp.int32, (TK, TQ), 0)
    qry = lax.broadcasted_iota(jnp.int32, (TK, TQ), 1)
    causal = key <= qry
    step(2 * qi, 0, keep=jnp.concatenate([causal] * 2, axis=1), next_cols=late_cols)
    step(2 * qi + 1, 1, keep=jnp.concatenate([causal[:, 0:TQ - TK]] * 2, axis=1),
         cols=late_cols, next_cols=None)

    lam = (jnp.exp(jnp.sum(lq1_ref[...] * lk1_ref[...], axis=-1, keepdims=True))
           - jnp.exp(jnp.sum(lq2_ref[...] * lk2_ref[...], axis=-1, keepdims=True))
           + lambda_init)
    for h in heads:
        o = acc_scr[h, 0:HEAD_W, :] / acc_scr[h, HEAD_W:HEAD_W + 1, :]
        o = o[:, 0:TQ] - lam * o[:, TQ:2 * TQ]
        o = o * lax.rsqrt(jnp.mean(o * o, axis=0, keepdims=True) + LN_EPS)
        o_ref[:, head_cols(h)] = (o.T * sg_ref[...] * (1.0 - lambda_init)).astype(BF16)


def _diff_attention(qt, k, vt, lq1, lk1, lq2, lk2, sg, batch, seq, lambda_init):
    n = k.shape[0]
    steps = seq // TQ
    qt_spec = pl.BlockSpec((None, HALF, TQ), lambda b, i: (b * steps + i, 0, 0))
    k_spec = pl.BlockSpec((seq, HALF), lambda b, i: (b, 0))
    vt_spec = pl.BlockSpec((seq // TK, HALF, TK), lambda b, i: (b, 0, 0))
    o_spec = pl.BlockSpec((TQ, HALF), lambda b, i: (b * steps + i, 0))
    return pl.pallas_call(
        functools.partial(_attn_kernel, lambda_init=lambda_init),
        grid=(batch, steps),
        in_specs=[qt_spec, k_spec, vt_spec, _const_spec(lq1.shape), _const_spec(lk1.shape),
                  _const_spec(lq2.shape), _const_spec(lk2.shape), _const_spec(sg.shape)],
        out_specs=o_spec,
        out_shape=jax.ShapeDtypeStruct((n, HALF), BF16),
        scratch_shapes=[pltpu.VMEM((C_HEADS, HEAD_W, 2 * TQ), BF16),
                        pltpu.VMEM((2, C_HEADS, TK, 2 * TQ), F32),
                        pltpu.VMEM((2, C_HEADS, 1, 2 * TQ), F32),
                        pltpu.VMEM((C_HEADS, 1, 2 * TQ), F32),
                        pltpu.VMEM((C_HEADS, HEAD_W + ONES_ROWS, 2 * TQ), F32)],
        compiler_params=pltpu.CompilerParams(
            dimension_semantics=("arbitrary", "arbitrary"), vmem_limit_bytes=VMEM_LIMIT),
        name="diff_attention",
    )(qt, k, vt, lq1, lk1, lq2, lk2, sg)


def kernel(x, even_w_in, even_conv_a_w, even_conv_b_w, even_conv_b_bias, even_conv_ln_g, even_conv_ln_b, even_w_out, odd_w_in, odd_lambda_q1, odd_lambda_k1, odd_lambda_q2, odd_lambda_k2, odd_subln_g, odd_gmlp_ln_g, odd_gmlp_ln_b, odd_spatial_w, odd_spatial_b, odd_w_out, mix_ln_g, mix_ln_b, ffn_w_gate, ffn_w_up, ffn_w_down, ffn_ln_g, ffn_ln_b):
    batch, seq, d = x.shape
    assert d == D_MODEL and seq % TILE == 0 and ODD_TILE % TQ == 0 and TQ == 2 * TK
    xf = x.reshape(batch * seq, d)
    row = lambda a: a.reshape(1, -1)
    bf = lambda a: a.astype(BF16)

    w_gate, w_up, w_down = _to_bf16(ffn_w_gate), _to_bf16(ffn_w_up), _to_bf16(ffn_w_down)

    def post(layer, ya, yb, xres, w_out):
        return _post_block(layer, ya, yb, xres, bf(w_out), row(mix_ln_g[layer]),
                           row(mix_ln_b[layer]), w_gate, w_up, w_down,
                           row(ffn_ln_g[layer]), row(ffn_ln_b[layer]))

    ya, yb = _even_mixer(xf, even_w_in[0], even_conv_a_w[0], even_conv_b_w[0],
                         row(even_conv_b_bias[0]), row(even_conv_ln_g[0]), row(even_conv_ln_b[0]),
                         batch, seq)
    xf = post(0, ya, yb, xf, even_w_out[0])

    lambda_init = 0.8 - 0.6 * math.exp(-0.3 * 1)
    sb = jnp.broadcast_to(odd_spatial_b[0].T[:, :, None], (CHUNK, D_GROUPS, CHUNK)).reshape(CHUNK, HALF)
    qt, k, vt, yd = _odd_proj(xf, odd_w_in[0], row(odd_gmlp_ln_g[0]), row(odd_gmlp_ln_b[0]),
                              odd_spatial_w[0], sb)
    yc = _diff_attention(qt, k, vt, row(odd_lambda_q1[0]), row(odd_lambda_k1[0]),
                         row(odd_lambda_q2[0]), row(odd_lambda_k2[0]), row(odd_subln_g[0]),
                         batch, seq, lambda_init)
    xf = post(1, yc, yd, xf, odd_w_out[0])
    return xf.reshape(batch, seq, d)
```
